```python
import math
import jax, jax.numpy as jnp
from jax import lax
import numpy as np


D_MODEL = 1024
BATCH = 16
SEQ = 2048
DEPTH = 4
DEC_BATCH = 32
DEC_SEQ = 2048
PAST_LEN = 128

N_EVEN = (DEPTH + 1) // 2
N_ODD = DEPTH // 2
EPS = 1e-5

D_SSD = D_MODEL
SSD_HEAD_DIM = 64
SSD_HEADS = D_SSD // SSD_HEAD_DIM
SSD_GROUPS = 2
SSD_STATE = 64
SSD_CHUNK = 128
D_CONV = 5
CONV_CH = D_SSD + 2 * SSD_GROUPS * SSD_STATE

D_ATT = D_MODEL
DA_HEAD_DIM = 64
DA_HEADS = D_ATT // (2 * DA_HEAD_DIM)
Q_BLOCK = 128
N_BUCKETS = 32
MAX_DISTANCE = 128

IN_AB = D_SSD + CONV_CH + 2 * SSD_HEADS + 4 * D_ATT
D_AB = D_SSD + D_ATT

D_S5 = D_MODEL
S5_GROUP = 16
S5_GROUPS = D_S5 // S5_GROUP
S5_STATE = 64
IN_C = 2 * D_S5

kernel_name = 'hybrid_ssd_diffattn_s5_encoder'


def rmsnorm(x, w):
    xf = x.astype(jnp.float32)
    y = xf * lax.rsqrt(jnp.mean(xf * xf, axis=-1, keepdims=True) + EPS)
    return (y * w.astype(jnp.float32)).astype(x.dtype)


def rev(t):
    return jnp.flip(t, axis=1)


def depthwise_conv(x, w, b):
    c = x.shape[-1]
    y = lax.conv_general_dilated(x, w[:, None, :], window_strides=(1,),
                                 padding=[(D_CONV // 2, D_CONV // 2)],
                                 dimension_numbers=('NWC', 'WIO', 'NWC'),
                                 feature_group_count=c)
    return y + b


def ssd_scan(x, dt, A, Bm, Cm):
    b, L, H, P = x.shape
    G, N = Bm.shape[2], Bm.shape[3]
    R = H // G
    nc = L // SSD_CHUNK
    xdt = (x * dt[..., None]).reshape(b, nc, SSD_CHUNK, G, R, P)
    a = (dt * A).reshape(b, nc, SSD_CHUNK, G, R)
    Bc = Bm.reshape(b, nc, SSD_CHUNK, G, N)
    Cc = Cm.reshape(b, nc, SSD_CHUNK, G, N)
    a_cs = jnp.cumsum(a, axis=2)
    seg = a_cs[:, :, :, None] - a_cs[:, :, None]
    lower = jnp.tril(jnp.ones((SSD_CHUNK, SSD_CHUNK), dtype=bool))
    Lm = jnp.exp(jnp.where(lower[:, :, None, None], seg, -jnp.inf))
    cb = jnp.einsum('bcign,bcjgn->bcijg', Cc, Bc)
    y_diag = jnp.einsum('bcijg,bcijgr,bcjgrp->bcigrp', cb, Lm, xdt)
    decay_to_end = jnp.exp(a_cs[:, :, -1:] - a_cs)
    chunk_states = jnp.einsum('bcjgn,bcjgr,bcjgrp->bcgrpn', Bc, decay_to_end, xdt)
    chunk_decay = jnp.exp(a_cs[:, :, -1])

    def step(h, inp):
        s, d = inp
        return h * d[..., None, None] + s, h

    h0 = jnp.zeros_like(chunk_states[:, 0])
    _, h_prev = lax.scan(step, h0, (jnp.moveaxis(chunk_states, 1, 0), jnp.moveaxis(chunk_decay, 1, 0)))
    h_prev = jnp.moveaxis(h_prev, 0, 1)
    y_off = jnp.einsum('bcign,bcigr,bcgrpn->bcigrp', Cc, jnp.exp(a_cs), h_prev)
    return (y_diag + y_off).reshape(b, L, H, P)


def ssd_mixer(z, xbc, dt_raw, conv_w, conv_b, dt_bias, A_log, D_skip, norm_w):
    b, L, _ = z.shape
    GN = SSD_GROUPS * SSD_STATE
    xbc = jax.nn.silu(depthwise_conv(xbc, conv_w, conv_b))
    xs = xbc[..., :D_SSD].reshape(b, L, SSD_HEADS, SSD_HEAD_DIM)
    Bm = xbc[..., D_SSD:D_SSD + GN].reshape(b, L, SSD_GROUPS, SSD_STATE)
    Cm = xbc[..., D_SSD + GN:].reshape(b, L, SSD_GROUPS, SSD_STATE)
    dt_raw = dt_raw.reshape(b, L, 2, SSD_HEADS)
    y = xs * D_skip[:, None]
    for d in range(2):
        dt = jax.nn.softplus(dt_raw[:, :, d] + dt_bias[d])
        A = -jnp.exp(A_log[d])
        if d == 0:
            y = y + ssd_scan(xs, dt, A, Bm, Cm)
        else:
            y = y + rev(ssd_scan(rev(xs), rev(dt), A, rev(Bm), rev(Cm)))
    y = y.reshape(b, L, D_SSD)
    return rmsnorm(y * jax.nn.silu(z), norm_w)


def t5_bucket(rel):
    half = N_BUCKETS // 2
    max_exact = half // 2
    n = jnp.abs(rel)
    large = max_exact + (jnp.log(jnp.maximum(n, 1).astype(jnp.float32) / max_exact)
                         / math.log(MAX_DISTANCE / max_exact) * (half - max_exact)).astype(jnp.int32)
    large = jnp.minimum(large, half - 1)
    return jnp.where(rel > 0, half, 0) + jnp.where(n < max_exact, n, large)


def diff_attention(q, k, v, g, lam_qk, subln_w, rel_bias, lambda_init):
    b, L, _ = q.shape
    nb = L // Q_BLOCK
    scale = DA_HEAD_DIM ** -0.5
    qb_all = (q * scale).reshape(b, nb, Q_BLOCK, DA_HEADS, 2, DA_HEAD_DIM).transpose(1, 0, 2, 3, 4, 5)
    k = k.reshape(b, L, DA_HEADS, 2, DA_HEAD_DIM)
    v = v.reshape(b, L, DA_HEADS, 2 * DA_HEAD_DIM)
    lq = lam_qk.astype(jnp.float32)
    lam = jnp.exp(jnp.sum(lq[0] * lq[1])) - jnp.exp(jnp.sum(lq[2] * lq[3])) + lambda_init
    key_pos = jnp.arange(L)

    def block(args):
        i, qb = args
        s = jnp.einsum('bqhcd,bkhcd->bhcqk', qb, k).astype(jnp.float32)
        rel = key_pos[None, :] - (i * Q_BLOCK + jnp.arange(Q_BLOCK))[:, None]
        bias = rel_bias[t5_bucket(rel)].astype(jnp.float32)
        s = s + jnp.transpose(bias, (2, 0, 1))[None, :, None]
        p = jax.nn.softmax(s, axis=-1)
        w = (p[:, :, 0] - lam * p[:, :, 1]).astype(v.dtype)
        return jnp.einsum('bhqk,bkhe->bqhe', w, v)

    o = lax.map(block, (jnp.arange(nb), qb_all))
    o = o.transpose(1, 0, 2, 3, 4).reshape(b, L, DA_HEADS, 2 * DA_HEAD_DIM)
    o = rmsnorm(o, subln_w) * (1.0 - lambda_init)
    return o.reshape(b, L, D_ATT) * jax.nn.silu(g)


def complex_affine_combine(e1, e2):
    ar1, ai1, br1, bi1 = e1
    ar2, ai2, br2, bi2 = e2
    ar = ar1 * ar2 - ai1 * ai2
    ai = ar1 * ai2 + ai1 * ar2
    br = ar2 * br1 - ai2 * bi1 + br2
    bi = ar2 * bi1 + ai2 * br1 + bi2
    return ar, ai, br, bi


def s5_mixer(u, lam_re, lam_im, log_dt, B_re, B_im, C_re, C_im, D_skip, w_glu, b_glu):
    b, L, _ = u.shape
    ug = u.reshape(b, L, S5_GROUPS, S5_GROUP)
    y = u * D_skip
    for d in range(2):
        lr, li = lam_re[d], lam_im[d]
        step = jnp.exp(log_dt[d])[:, None]
        mag = jnp.exp(lr * step)
        ab_re = mag * jnp.cos(li * step)
        ab_im = mag * jnp.sin(li * step)
        den = lr * lr + li * li
        nr = ab_re - 1.0
        cr = (nr * lr + ab_im * li) / den
        ci = (ab_im * lr - nr * li) / den
        bb_re = cr[..., None] * B_re - ci[..., None] * B_im
        bb_im = cr[..., None] * B_im + ci[..., None] * B_re
        bu_re = jnp.einsum('blgc,gpc->lbgp', ug, bb_re)
        bu_im = jnp.einsum('blgc,gpc->lbgp', ug, bb_im)
        a_re = jnp.broadcast_to(ab_re, (L, 1) + ab_re.shape)
        a_im = jnp.broadcast_to(ab_im, (L, 1) + ab_im.shape)
        _, _, x_re, x_im = lax.associative_scan(complex_affine_combine, (a_re, a_im, bu_re, bu_im),
                                                reverse=(d == 1), axis=0)
        yd = jnp.einsum('lbgp,gcp->blgc', x_re, C_re[d]) - jnp.einsum('lbgp,gcp->blgc', x_im, C_im[d])
        y = y + yd.reshape(b, L, D_S5)
    h = jax.nn.gelu(y)
    return h * jax.nn.sigmoid(h @ w_glu + b_glu)


def setup_inputs(seed: int = 0) -> dict:
    key = jax.random.key(seed)
    ks = jax.random.split(key, 28)
    f32 = jnp.float32

    def nrm(k, shape, scale):
        return jax.random.normal(k, shape, f32) * scale

    def unif(k, shape, lo, hi):
        return jax.random.uniform(k, shape, f32, minval=lo, maxval=hi)

    x_prompt = nrm(ks[0], (BATCH, SEQ, D_MODEL), 1.0)
    x_sample = nrm(ks[1], (DEC_BATCH, DEC_SEQ, D_MODEL), 1.0)
    norm_w = 1.0 + nrm(ks[2], (DEPTH, D_MODEL), 0.02)
    final_norm_w = 1.0 + nrm(ks[3], (D_MODEL,), 0.02)
    rel_bias = nrm(ks[4], (N_BUCKETS, DA_HEADS), 0.5)
    w_in_ab = nrm(ks[5], (N_EVEN, D_MODEL, IN_AB), D_MODEL ** -0.5)
    conv_w = nrm(ks[6], (N_EVEN, D_CONV, CONV_CH), D_CONV ** -0.5)
    conv_b = nrm(ks[7], (N_EVEN, CONV_CH), 0.02)
    dt0 = jnp.exp(unif(ks[8], (N_EVEN, 2, SSD_HEADS), math.log(1e-3), math.log(1e-1)))
    ssd_dt_bias = dt0 + jnp.log(-jnp.expm1(-dt0))
    ssd_A_log = jnp.log(unif(ks[9], (N_EVEN, 2, SSD_HEADS), 1.0, 16.0))
    ssd_D = 1.0 + nrm(ks[10], (N_EVEN, SSD_HEADS), 0.02)
    ssd_norm_w = 1.0 + nrm(ks[11], (N_EVEN, D_SSD), 0.02)
    diff_lambda = nrm(ks[12], (N_EVEN, 4, DA_HEAD_DIM), 0.1)
    diff_subln_w = 1.0 + nrm(ks[13], (N_EVEN, 2 * DA_HEAD_DIM), 0.02)
    w_out_ab = nrm(ks[14], (N_EVEN, D_AB, D_MODEL), D_AB ** -0.5)
    w_in_c = nrm(ks[15], (N_ODD, D_MODEL, IN_C), D_MODEL ** -0.5)
    n_idx = jnp.arange(S5_STATE, dtype=f32)
    s5_lambda_re = -0.5 + nrm(ks[16], (N_ODD, 2, S5_GROUPS, S5_STATE), 0.01)
    s5_lambda_im = math.pi * n_idx + nrm(ks[17], (N_ODD, 2, S5_GROUPS, S5_STATE), 0.01)
    s5_log_dt = unif(ks[18], (N_ODD, 2, S5_GROUPS), math.log(1e-3), math.log(1e-1))
    s5_B_re = nrm(ks[19], (N_ODD, S5_GROUPS, S5_STATE, S5_GROUP), (2 * S5_GROUP) ** -0.5)
    s5_B_im = nrm(ks[20], (N_ODD, S5_GROUPS, S5_STATE, S5_GROUP), (2 * S5_GROUP) ** -0.5)
    s5_C_re = nrm(ks[21], (N_ODD, 2, S5_GROUPS, S5_GROUP, S5_STATE), S5_STATE ** -0.5)
    s5_C_im = nrm(ks[22], (N_ODD, 2, S5_GROUPS, S5_GROUP, S5_STATE), S5_STATE ** -0.5)
    s5_D = 1.0 + nrm(ks[23], (N_ODD, D_S5), 0.1)
    w_glu = nrm(ks[24], (N_ODD, D_S5, D_S5), D_S5 ** -0.5)
    b_glu = nrm(ks[25], (N_ODD, D_S5), 0.02)
    w_out_c = nrm(ks[26], (N_ODD, D_S5, D_MODEL), D_S5 ** -0.5)
    return {'x_prompt': x_prompt, 'x_sample': x_sample, 'norm_w': norm_w, 'final_norm_w': final_norm_w,
            'rel_bias': rel_bias, 'w_in_ab': w_in_ab, 'conv_w': conv_w, 'conv_b': conv_b,
            'ssd_dt_bias': ssd_dt_bias, 'ssd_A_log': ssd_A_log, 'ssd_D': ssd_D, 'ssd_norm_w': ssd_norm_w,
            'diff_lambda': diff_lambda, 'diff_subln_w': diff_subln_w, 'w_out_ab': w_out_ab,
            'w_in_c': w_in_c, 's5_lambda_re': s5_lambda_re, 's5_lambda_im': s5_lambda_im,
            's5_log_dt': s5_log_dt, 's5_B_re': s5_B_re, 's5_B_im': s5_B_im, 's5_C_re': s5_C_re,
            's5_C_im': s5_C_im, 's5_D': s5_D, 'w_glu': w_glu, 'b_glu': b_glu, 'w_out_c': w_out_c}


def reference(x_prompt, x_sample, norm_w, final_norm_w, rel_bias, w_in_ab, conv_w, conv_b,
              ssd_dt_bias, ssd_A_log, ssd_D, ssd_norm_w, diff_lambda, diff_subln_w, w_out_ab,
              w_in_c, s5_lambda_re, s5_lambda_im, s5_log_dt, s5_B_re, s5_B_im, s5_C_re, s5_C_im,
              s5_D, w_glu, b_glu, w_out_c):
    split_ab = [D_SSD, D_SSD + CONV_CH, D_SSD + CONV_CH + 2 * SSD_HEADS]
    split_ab = split_ab + [split_ab[-1] + D_ATT * i for i in range(1, 4)]

    def run(x):
        for l in range(DEPTH):
            h = rmsnorm(x, norm_w[l])
            if l % 2 == 0:
                e = l // 2
                proj = h @ w_in_ab[e]
                z, xbc, dt_raw, q, k, v, g = jnp.split(proj, split_ab, axis=-1)
                y_ssd = ssd_mixer(z, xbc, dt_raw, conv_w[e], conv_b[e], ssd_dt_bias[e], ssd_A_log[e],
                                  ssd_D[e], ssd_norm_w[e])
                lambda_init = 0.8 - 0.6 * math.exp(-0.3 * l)
                y_att = diff_attention(q, k, v, g, diff_lambda[e], diff_subln_w[e], rel_bias, lambda_init)
                x = x + jnp.concatenate([y_ssd, y_att], axis=-1) @ w_out_ab[e]
            else:
                o = l // 2
                proj = h @ w_in_c[o]
                u, zc = proj[..., :D_S5], proj[..., D_S5:]
                y = s5_mixer(u, s5_lambda_re[o], s5_lambda_im[o], s5_log_dt[o], s5_B_re[o], s5_B_im[o],
                             s5_C_re[o], s5_C_im[o], s5_D[o], w_glu[o], b_glu[o])
                x = x + (y * jax.nn.silu(zc)) @ w_out_c[o]
        return rmsnorm(x, final_norm_w)

    y_prompt = run(x_prompt)
    y_sample = run(x_sample)
    return (y_prompt, y_sample)
```

```python
import functools
import math

import jax
import jax.numpy as jnp
from jax import lax
from jax.experimental import pallas as pl
from jax.experimental.pallas import tpu as pltpu

F32 = jnp.float32
BF16 = jnp.bfloat16

D_MODEL = 1024
DEPTH = 4
EPS = 1e-5

SSD_HEADS = 16
SSD_HEAD_DIM = 64
SSD_GROUPS = 2
SSD_STATE = 64
D_CONV = 5
CONV_CH = D_MODEL + 2 * SSD_GROUPS * SSD_STATE
SSD_Q = 128
CONV_HALO = 16

DA_HEADS = 8
DA_HEAD_DIM = 64
N_BUCKETS = 32
MAX_DISTANCE = 128
ATT_QB = 256
BIAS_TILE = 128

S5_GROUP = 16
S5_GROUPS = 64
S5_STATE = 64
S5_T = 128

TM_MAX = 512
VMEM_LIMIT = 56 * 1024 * 1024


def _cparams(sem):
    return pltpu.CompilerParams(dimension_semantics=sem, vmem_limit_bytes=VMEM_LIMIT)


def _silu(x):
    return x * jax.nn.sigmoid(x)


def _rms_rows(x, w):
    ms = jnp.mean(x * x, axis=-1, keepdims=True)
    return x * lax.rsqrt(ms + EPS) * w


EVEN_SPLITS = (("z", 0, 1024), ("xbc", 1024, 1280), ("q", 2304, 1024), ("k", 3328, 1024),
               ("v", 4352, 1024), ("g", 5376, 1024))
EVEN_MAIN = 6400


def _proj_even_kernel(x_ref, nw_ref, wm_ref, wdt_ref, wdtT_ref,
                      z_ref, xbc_ref, q_ref, k_ref, v_ref, g_ref, dt_ref, dtT_ref):
    h = _rms_rows(x_ref[...], nw_ref[...]).astype(BF16)
    outs = (z_ref, xbc_ref, q_ref, k_ref, v_ref, g_ref)
    for ref, (_, off, n) in zip(outs, EVEN_SPLITS):
        ref[...] = jnp.dot(h, wm_ref[:, off:off + n], preferred_element_type=F32).astype(BF16)
    dt_ref[...] = jnp.dot(h, wdt_ref[...], preferred_element_type=F32)
    dtT_ref[...] = lax.dot_general(wdtT_ref[...], h, (((1,), (1,)), ((), ())),
                                   preferred_element_type=F32)


def _proj_even(x2d, nw, wm, wdt, wdtT):
    T = x2d.shape[0]
    TM = min(TM_MAX, T)
    row = lambda n: pl.BlockSpec((TM, n), lambda i: (i, 0))
    full = lambda a: pl.BlockSpec(a.shape, lambda i: (0,) * a.ndim)
    out_shape = [jax.ShapeDtypeStruct((T, n), BF16) for (_, _, n) in EVEN_SPLITS]
    out_shape += [jax.ShapeDtypeStruct((T, 32), F32), jax.ShapeDtypeStruct((32, T), F32)]
    out_specs = [row(n) for (_, _, n) in EVEN_SPLITS]
    out_specs += [row(32), pl.BlockSpec((32, TM), lambda i: (0, i))]
    return pl.pallas_call(
        _proj_even_kernel,
        grid=(T // TM,),
        in_specs=[row(D_MODEL), full(nw), full(wm), full(wdt), full(wdtT)],
        out_specs=out_specs,
        out_shape=out_shape,
        compiler_params=_cparams(("parallel",)),
        name="proj_even",
    )(x2d, nw, wm, wdt, wdtT)


def _ssd_kernel(z_ref, xbc_ref, dt_ref, dtT_ref, cw_ref, cb_ref, dtb_ref, dtbT_ref,
                alog_ref, alogT_ref, dexp_ref, nw_ref, o_ref,
                xpad, xconv, hb_all, hf_st, hb_st, *, L):
    Q = SSD_Q
    H = SSD_HEADS
    nc = L // Q
    NP = H // 2

    zeros_h = jnp.zeros((CONV_HALO, CONV_CH), BF16)
    xpad[0:CONV_HALO, :] = zeros_h
    xpad[L + CONV_HALO:L + 2 * CONV_HALO, :] = zeros_h

    def stage_body(c, carry):
        t0 = pl.multiple_of(c * Q, Q)
        xpad[pl.ds(t0 + CONV_HALO, Q), :] = xbc_ref[pl.ds(t0, Q), :]
        return carry
    lax.fori_loop(0, nc, stage_body, 0)

    def conv_body(c, carry):
        t0 = pl.multiple_of(c * Q, Q)
        slab = xpad[pl.ds(t0, Q + 2 * CONV_HALO), :].astype(F32)
        acc = jnp.zeros((Q, CONV_CH), F32) + cb_ref[...]
        for kk in range(D_CONV):
            sh = (D_CONV // 2 - kk) % (Q + 2 * CONV_HALO)
            r = slab if sh == 0 else pltpu.roll(slab, sh, axis=0)
            acc = acc + r[CONV_HALO:CONV_HALO + Q, :] * cw_ref[kk:kk + 1, :]
        xconv[pl.ds(t0, Q), :] = _silu(acc).astype(BF16)
        return carry
    lax.fori_loop(0, nc, conv_body, 0)

    ri = lax.broadcasted_iota(jnp.int32, (Q, Q), 0)
    ci = lax.broadcasted_iota(jnp.int32, (Q, Q), 1)
    lower_incl = (ri >= ci).astype(F32)
    upper_incl = (ci >= ri).astype(F32)
    strict_lower = ri > ci
    eye = ri == ci
    lane128 = lax.broadcasted_iota(jnp.int32, (Q, 128), 1)
    lo_half = lane128 < 64
    col32 = lax.broadcasted_iota(jnp.int32, (Q, 2 * H), 1)
    row32 = lax.broadcasted_iota(jnp.int32, (2 * H, Q), 0)
    A_col = -jnp.exp(alog_ref[...])
    A_row = -jnp.exp(alogT_ref[...])
    hi = lax.Precision.HIGHEST

    def chunk_scalars(t0):
        dt_c = jax.nn.softplus(dt_ref[pl.ds(t0, Q), :] + dtb_ref[...])
        a_c = dt_c * A_col
        pre_c = jnp.dot(lower_incl, a_c, precision=hi, preferred_element_type=F32)
        suf_c = jnp.dot(upper_incl, a_c, precision=hi, preferred_element_type=F32)
        cs_c = jnp.where(col32 < H, pre_c, suf_c)
        dt_r = jax.nn.softplus(dtT_ref[:, pl.ds(t0, Q)] + dtbT_ref[...])
        a_r = dt_r * A_row
        pre_r = jnp.dot(a_r, upper_incl, precision=hi, preferred_element_type=F32)
        suf_r = jnp.dot(a_r, lower_incl, precision=hi, preferred_element_type=F32)
        cs_r = jnp.where(row32 < H, pre_r, suf_r)
        return dt_c, cs_c, dt_r, cs_r

    def expand_pairs(small, base):
        parts = []
        for p in range(NP):
            c0 = small[:, base + 2 * p:base + 2 * p + 1]
            c1 = small[:, base + 2 * p + 1:base + 2 * p + 2]
            parts.append(jnp.where(lo_half, c0, c1))
        return jnp.concatenate(parts, axis=1)

    def group_mask(x128, g):
        keep = lo_half if g == 0 else jnp.logical_not(lo_half)
        return jnp.where(keep, x128, jnp.zeros_like(x128))

    def state_update(st_ref, Bm, xs_scaled, decay_row):
        for g in range(SSD_GROUPS):
            Bg = group_mask(Bm, g)
            dS = lax.dot_general(Bg, xs_scaled[:, g * 512:(g + 1) * 512],
                                 (((0,), (0,)), ((), ())), preferred_element_type=F32)
            st_ref[g] = st_ref[g] * decay_row[:, g * 512:(g + 1) * 512] + dS

    hb_st[...] = jnp.zeros_like(hb_st)

    def bwd_body(cc, carry):
        c = nc - 1 - cc
        t0 = pl.multiple_of(c * Q, Q)
        dt_c, cs_c, _, _ = chunk_scalars(t0)
        hb_all[c] = hb_st[...].astype(BF16)
        tot = cs_c[0:1, :]
        w_state = jnp.exp(tot - cs_c) * dt_c
        e4 = expand_pairs(w_state, H)
        decay = expand_pairs(jnp.exp(cs_c), H)[0:1, :]
        xs = xconv[pl.ds(t0, Q), 0:D_MODEL].astype(F32)
        Bm = xconv[pl.ds(t0, Q), D_MODEL:D_MODEL + 128]
        state_update(hb_st, Bm, (xs * e4).astype(BF16), decay)
        return carry
    lax.fori_loop(0, nc, bwd_body, 0)

    hf_st[...] = jnp.zeros_like(hf_st)

    def fwd_body(c, carry):
        t0 = pl.multiple_of(c * Q, Q)
        dt_c, cs_c, dt_r, cs_r = chunk_scalars(t0)
        xs_b = xconv[pl.ds(t0, Q), 0:D_MODEL]
        xs = xs_b.astype(F32)
        Bm = xconv[pl.ds(t0, Q), D_MODEL:D_MODEL + 128]
        Cm = xconv[pl.ds(t0, Q), D_MODEL + 128:D_MODEL + 256]
        ecs = jnp.exp(cs_c)
        e1 = expand_pairs(ecs, 0)
        e2 = expand_pairs(ecs, H)
        rterm = jnp.log(dt_r) - cs_r

        y = xs * dexp_ref[...]
        for g in range(SSD_GROUPS):
            Cg = group_mask(Cm, g)
            cbm = lax.dot_general(Cg, Bm, (((1,), (1,)), ((), ())),
                                  preferred_element_type=F32)
            yoff_f = jnp.dot(Cg, hf_st[g].astype(BF16), preferred_element_type=F32)
            yoff_b = jnp.dot(Cg, hb_all[c, g], preferred_element_type=F32)
            sl = slice(g * 512, (g + 1) * 512)
            ydiag = []
            for pp in range(NP // SSD_GROUPS):
                mats = []
                for hh in range(2):
                    h = g * (H // SSD_GROUPS) + 2 * pp + hh
                    ef = cs_c[:, h:h + 1] + rterm[h:h + 1, :]
                    eb = cs_c[:, H + h:H + h + 1] + rterm[H + h:H + h + 1, :]
                    w = jnp.exp(jnp.where(strict_lower, ef, eb))
                    w = w + jnp.where(eye, dt_r[h:h + 1, :], 0.0)
                    mats.append((cbm * w).astype(BF16))
                lhs = jnp.concatenate(mats, axis=1)
                p_abs = g * (NP // SSD_GROUPS) + pp
                xp = xs_b[:, p_abs * 128:(p_abs + 1) * 128]
                rhs = jnp.concatenate(
                    [jnp.where(lo_half, xp, jnp.zeros_like(xp)),
                     jnp.where(lo_half, jnp.zeros_like(xp), xp)], axis=0)
                ydiag.append(jnp.dot(lhs, rhs, preferred_element_type=F32))
            ydiag = jnp.concatenate(ydiag, axis=1)
            y_g = ydiag + yoff_f * e1[:, sl] + yoff_b * e2[:, sl]
            y = y + jnp.concatenate(
                [y_g, jnp.zeros_like(y_g)] if g == 0 else [jnp.zeros_like(y_g), y_g], axis=1)

        tot = cs_c[Q - 1:Q, :]
        w_state = jnp.exp(tot - cs_c) * dt_c
        e3 = expand_pairs(w_state, 0)
        state_update(hf_st, Bm, (xs * e3).astype(BF16), e1[Q - 1:Q, :])

        zg = z_ref[pl.ds(t0, Q), :].astype(F32)
        o_ref[pl.ds(t0, Q), :] = _rms_rows(y * _silu(zg), nw_ref[...]).astype(BF16)
        return carry
    lax.fori_loop(0, nc, fwd_body, 0)


def _ssd(z, xbc, dt, dtT, cw, cb, dtb, dtbT, alog, alogT, dexp, nw, *, nb, L):
    full = lambda a: pl.BlockSpec(a.shape, lambda b: (0,) * a.ndim)
    nc = L // SSD_Q
    return pl.pallas_call(
        functools.partial(_ssd_kernel, L=L),
        grid=(nb,),
        in_specs=[pl.BlockSpec((L, D_MODEL), lambda b: (b, 0)),
                  pl.BlockSpec((L, CONV_CH), lambda b: (b, 0)),
                  pl.BlockSpec((L, 32), lambda b: (b, 0)),
                  pl.BlockSpec((32, L), lambda b: (0, b)),
                  full(cw), full(cb), full(dtb), full(dtbT), full(alog), full(alogT),
                  full(dexp), full(nw)],
        out_specs=pl.BlockSpec((L, D_MODEL), lambda b: (b, 0)),
        out_shape=jax.ShapeDtypeStruct((nb * L, D_MODEL), BF16),
        scratch_shapes=[pltpu.VMEM((L + 2 * CONV_HALO, CONV_CH), BF16),
                        pltpu.VMEM((L, CONV_CH), BF16),
                        pltpu.VMEM((nc, SSD_GROUPS, 128, 512), BF16),
                        pltpu.VMEM((SSD_GROUPS, 128, 512), F32),
                        pltpu.VMEM((SSD_GROUPS, 128, 512), F32)],
        compiler_params=_cparams(("parallel",)),
        name="ssd_mixer",
    )(z, xbc, dt, dtT, cw, cb, dtb, dtbT, alog, alogT, dexp, nw)


def _attn_kernel(q_ref, k_ref, v_ref, g_ref, bt_ref, lq_ref, sw_ref, o_ref, *, L, lambda_init):
    QB = ATT_QB
    nbt = L // BIAS_TILE
    k = k_ref[...]
    v = v_ref[...]
    v_ext = jnp.concatenate([v, jnp.ones_like(v)], axis=1)
    lq = lq_ref[...]
    lam = (jnp.exp(jnp.sum(lq[0:1] * lq[1:2], axis=-1, keepdims=True))
           - jnp.exp(jnp.sum(lq[2:3] * lq[3:4], axis=-1, keepdims=True)) + lambda_init)
    lane = lax.broadcasted_iota(jnp.int32, (QB, 128), 1)
    lo_half = lane < 64
    scale = DA_HEAD_DIM ** -0.5

    for qi in range(L // QB):
        q = q_ref[qi * QB:(qi + 1) * QB, :] * scale
        bias = jnp.concatenate(
            [bt_ref[0, :, (nbt - 1 - (qi * (QB // BIAS_TILE) + r)) * BIAS_TILE:
                    (nbt - 1 - (qi * (QB // BIAS_TILE) + r)) * BIAS_TILE + L]
             for r in range(QB // BIAS_TILE)], axis=0)
        outs = []
        for c in range(2):
            qc = jnp.where(lo_half if c == 0 else jnp.logical_not(lo_half), q, jnp.zeros_like(q))
            s = lax.dot_general(qc, k, (((1,), (1,)), ((), ())), preferred_element_type=F32)
            s = s + bias
            m = jnp.max(s, axis=-1, keepdims=True)
            e = jnp.exp(s - m).astype(BF16)
            pv = jnp.dot(e, v_ext, preferred_element_type=F32)
            outs.append(pv[:, 0:128] / pv[:, 128:256])
        o = outs[0] - lam * outs[1]
        o = _rms_rows(o, sw_ref[...]) * (1.0 - lambda_init)
        gg = g_ref[qi * QB:(qi + 1) * QB, :].astype(F32)
        o_ref[qi * QB:(qi + 1) * QB, :] = (o * _silu(gg)).astype(BF16)


def _attn(q, k, v, g, bt, lq, sw, *, nb, L, lambda_init):
    blk = pl.BlockSpec((L, 128), lambda b, h: (b, h))
    return pl.pallas_call(
        functools.partial(_attn_kernel, L=L, lambda_init=lambda_init),
        grid=(nb, DA_HEADS),
        in_specs=[blk, blk, blk, blk,
                  pl.BlockSpec((1,) + bt.shape[1:], lambda b, h: (h, 0, 0)),
                  pl.BlockSpec(lq.shape, lambda b, h: (0, 0)),
                  pl.BlockSpec(sw.shape, lambda b, h: (0, 0))],
        out_specs=blk,
        out_shape=jax.ShapeDtypeStruct((nb * L, D_MODEL), BF16),
        compiler_params=_cparams(("parallel", "parallel")),
        name="diff_attn",
    )(q, k, v, g, bt, lq, sw)


def _t5_bucket(rel):
    half = N_BUCKETS // 2
    max_exact = half // 2
    n = jnp.abs(rel)
    large = max_exact + (jnp.log(jnp.maximum(n, 1).astype(F32) / max_exact)
                         / math.log(MAX_DISTANCE / max_exact) * (half - max_exact)).astype(jnp.int32)
    large = jnp.minimum(large, half - 1)
    return jnp.where(rel > 0, half, 0) + jnp.where(n < max_exact, n, large)


def _bias_tiles(rel_bias, L):
    nbt = L // BIAS_TILE
    r = jnp.arange(BIAS_TILE)[:, None]
    c = jnp.arange((2 * nbt - 1) * BIAS_TILE)[None, :]
    rel = c - r - (nbt - 1) * BIAS_TILE
    return jnp.transpose(rel_bias[_t5_bucket(rel)].astype(F32), (2, 0, 1))


def _out_even_kernel(a_ref, b_ref, w_ref, x_ref, o_ref):
    acc = jnp.dot(a_ref[...], w_ref[0:D_MODEL, :], preferred_element_type=F32)
    acc = acc + jnp.dot(b_ref[...], w_ref[D_MODEL:2 * D_MODEL, :], preferred_element_type=F32)
    o_ref[...] = x_ref[...] + acc


def _out_even(a, b, w, x2d):
    T = x2d.shape[0]
    TM = min(TM_MAX, T)
    row = pl.BlockSpec((TM, D_MODEL), lambda i: (i, 0))
    return pl.pallas_call(
        _out_even_kernel,
        grid=(T // TM,),
        in_specs=[row, row, pl.BlockSpec(w.shape, lambda i: (0, 0)), row],
        out_specs=row,
        out_shape=jax.ShapeDtypeStruct((T, D_MODEL), F32),
        compiler_params=_cparams(("parallel",)),
        name="out_even",
    )(a, b, w, x2d)


def _proj_odd_kernel(x_ref, nw_ref, wuT_ref, wz_ref, uT_ref, zc_ref):
    h = _rms_rows(x_ref[...], nw_ref[...]).astype(BF16)
    uT_ref[...] = lax.dot_general(wuT_ref[...], h, (((1,), (1,)), ((), ())),
                                  preferred_element_type=F32).astype(BF16)
    zc_ref[...] = jnp.dot(h, wz_ref[...], preferred_element_type=F32).astype(BF16)


def _proj_odd(x2d, nw, wuT, wz):
    T = x2d.shape[0]
    TM = min(TM_MAX, T)
    row = pl.BlockSpec((TM, D_MODEL), lambda i: (i, 0))
    full = lambda a: pl.BlockSpec(a.shape, lambda i: (0,) * a.ndim)
    return pl.pallas_call(
        _proj_odd_kernel,
        grid=(T // TM,),
        in_specs=[row, full(nw), full(wuT), full(wz)],
        out_specs=[pl.BlockSpec((D_MODEL, TM), lambda i: (0, i)), row],
        out_shape=[jax.ShapeDtypeStruct((D_MODEL, T), BF16),
                   jax.ShapeDtypeStruct((T, D_MODEL), BF16)],
        compiler_params=_cparams(("parallel",)),
        name="proj_odd",
    )(x2d, nw, wuT, wz)


def _s5_kernel(x_ref, w1_ref, q_ref, a1_ref, a2_ref, d_ref, o_ref, s_scr, h_scr, *, nchunk):
    GS = S5_GROUP
    R = x_ref.shape[1]
    nseq = R // nchunk
    KT = GS * S5_T
    lhs = jnp.concatenate([x_ref[c] for c in range(GS)], axis=1)
    acc = jnp.dot(lhs, w1_ref[0], preferred_element_type=F32)
    s_scr[0] = acc[:, KT:KT + 128]
    s_scr[1] = acc[:, KT + 128:KT + 256]
    a1 = a1_ref[0]
    a2 = a2_ref[0]

    def run(d, order):
        a1d, a2d = a1[:, d * 128:(d + 1) * 128], a2[:, d * 128:(d + 1) * 128]
        h = jnp.zeros((nseq, 128), F32)
        for kk in order:
            rows = pl.ds(kk, nseq, stride=nchunk)
            h_scr[d, rows, :] = h
            h = h * a1d + pltpu.roll(h, 64, axis=1) * a2d + s_scr[d, rows, :]
    run(0, range(nchunk))
    run(1, range(nchunk - 1, -1, -1))

    hprev = jnp.concatenate([h_scr[0], h_scr[1]], axis=1).astype(BF16)
    y = acc[:, :KT] + jnp.dot(hprev, q_ref[0], preferred_element_type=F32)
    for c in range(GS):
        skip = d_ref[0, c:c + 1, :] * x_ref[c].astype(F32)
        o_ref[c] = (y[:, c * S5_T:(c + 1) * S5_T] + skip).astype(BF16)


def _s5(uT3, w1, qm, a1, a2, dexp, *, nchunk):
    R = uT3.shape[1]
    xblk = pl.BlockSpec((S5_GROUP, R, S5_T), lambda g: (g, 0, 0))
    per_g = lambda a: pl.BlockSpec((1,) + a.shape[1:], lambda g: (g,) + (0,) * (a.ndim - 1))
    return pl.pallas_call(
        functools.partial(_s5_kernel, nchunk=nchunk),
        grid=(S5_GROUPS,),
        in_specs=[xblk, per_g(w1), per_g(qm), per_g(a1), per_g(a2), per_g(dexp)],
        out_specs=xblk,
        out_shape=jax.ShapeDtypeStruct(uT3.shape, BF16),
        scratch_shapes=[pltpu.VMEM((2, R, 128), F32), pltpu.VMEM((2, R, 128), F32)],
        compiler_params=_cparams(("parallel",)),
        name="s5_mixer",
    )(uT3, w1, qm, a1, a2, dexp)


def _s5_weights(lam_re, lam_im, log_dt, B_re, B_im, C_re, C_im):
    T = S5_T
    hp = lax.Precision.HIGHEST
    step = jnp.exp(log_dt)[:, :, None]
    lr, li = lam_re, lam_im
    t = jnp.arange(T + 1, dtype=F32)[:, None, None, None]
    pw_mag = jnp.exp(lr * step * t)
    pw_re = pw_mag * jnp.cos(li * step * t)
    pw_im = pw_mag * jnp.sin(li * step * t)
    ab_re, ab_im = pw_re[1], pw_im[1]
    den = lr * lr + li * li
    nr = ab_re - 1.0
    cr = (nr * lr + ab_im * li) / den
    ci = (ab_im * lr - nr * li) / den
    bb_re = cr[..., None] * B_re - ci[..., None] * B_im
    bb_im = cr[..., None] * B_im + ci[..., None] * B_re
    ca_re = C_re[None] * pw_re[:, :, :, None, :] - C_im[None] * pw_im[:, :, :, None, :]
    ca_im = C_re[None] * pw_im[:, :, :, None, :] + C_im[None] * pw_re[:, :, :, None, :]
    kk = (jnp.einsum('tdgcp,dgpk->tdgck', ca_re[:T], bb_re, precision=hp)
          - jnp.einsum('tdgcp,dgpk->tdgck', ca_im[:T], bb_im, precision=hp))
    kf, kb = kk[:, 0], kk[:, 1]
    lagged = jnp.concatenate([kb[:0:-1], (kf[0] + kb[0])[None], kf[1:]], axis=0)
    lagged = jnp.transpose(lagged, (1, 3, 2, 0)).astype(BF16)
    idx = jnp.arange(T)[None, :] - jnp.arange(T)[:, None] + (T - 1)
    toe = lagged[:, :, :, idx]
    G = toe.shape[0]
    toe = jnp.transpose(toe, (0, 1, 3, 2, 4)).reshape(G, S5_GROUP * T, S5_GROUP * T)
    pf_re, pf_im = pw_re[:T][::-1, 0], pw_im[:T][::-1, 0]
    pb_re, pb_im = pw_re[:T, 1], pw_im[:T, 1]

    def state_proj(p_re, p_im, d):
        s_re = p_re[:, :, :, None] * bb_re[d][None] - p_im[:, :, :, None] * bb_im[d][None]
        s_im = p_re[:, :, :, None] * bb_im[d][None] + p_im[:, :, :, None] * bb_re[d][None]
        s = jnp.concatenate([s_re, s_im], axis=2)
        return jnp.transpose(s, (1, 3, 0, 2))
    pm = jnp.concatenate([state_proj(pf_re, pf_im, 0), state_proj(pb_re, pb_im, 1)], axis=-1)
    pm = pm.reshape(G, S5_GROUP * T, 4 * S5_STATE).astype(BF16)
    w1 = jnp.concatenate([toe, pm], axis=-1)
    def out_proj(re, im):
        o = jnp.concatenate([re, -im], axis=-1)
        return jnp.transpose(o, (1, 3, 2, 0))
    qf = out_proj(ca_re[1:T + 1, 0], ca_im[1:T + 1, 0])
    qb = out_proj(ca_re[T:0:-1, 1], ca_im[T:0:-1, 1])
    qm = jnp.concatenate([qf, qb], axis=1).reshape(G, 4 * S5_STATE, S5_GROUP * T).astype(BF16)
    at_re, at_im = pw_re[T], pw_im[T]
    a1 = jnp.concatenate([at_re[0], at_re[0], at_re[1], at_re[1]], axis=-1)[:, None, :]
    a2 = jnp.concatenate([-at_im[0], at_im[0], -at_im[1], at_im[1]], axis=-1)[:, None, :]
    return w1, qm, a1, a2


def _out_odd_kernel(y_ref, zc_ref, x_ref, wg_ref, bg_ref, wo_ref, fw_ref, o_ref, *, final):
    h = jax.nn.gelu(y_ref[...].astype(F32))
    gate = jax.nn.sigmoid(jnp.dot(h.astype(BF16), wg_ref[...], preferred_element_type=F32)
                          + bg_ref[...])
    t = h * gate * _silu(zc_ref[...].astype(F32))
    xn = x_ref[...] + jnp.dot(t.astype(BF16), wo_ref[...], preferred_element_type=F32)
    o_ref[...] = _rms_rows(xn, fw_ref[...]) if final else xn


def _out_odd(y, zc, x2d, wg, bg, wo, fw, *, final):
    T = x2d.shape[0]
    TM = min(TM_MAX, T)
    row = pl.BlockSpec((TM, D_MODEL), lambda i: (i, 0))
    full = lambda a: pl.BlockSpec(a.shape, lambda i: (0,) * a.ndim)
    return pl.pallas_call(
        functools.partial(_out_odd_kernel, final=final),
        grid=(T // TM,),
        in_specs=[row, row, row, full(wg), full(bg), full(wo), full(fw)],
        out_specs=row,
        out_shape=jax.ShapeDtypeStruct((T, D_MODEL), F32),
        compiler_params=_cparams(("parallel",)),
        name="out_odd",
    )(y, zc, x2d, wg, bg, wo, fw)


def _prep_even(e, w_in_ab, conv_w, conv_b, ssd_dt_bias, ssd_A_log, ssd_D, ssd_norm_w,
               diff_lambda, diff_subln_w, w_out_ab):
    w = w_in_ab[e]
    o_dt = D_MODEL + CONV_CH
    wm = jnp.concatenate([w[:, :o_dt], w[:, o_dt + 2 * SSD_HEADS:]], axis=1).astype(BF16)
    wdt = w[:, o_dt:o_dt + 2 * SSD_HEADS].astype(BF16)
    return dict(
        wm=wm, wdt=wdt, wdtT=wdt.T,
        cw=conv_w[e], cb=conv_b[e][None, :],
        dtb=ssd_dt_bias[e].reshape(1, -1), dtbT=ssd_dt_bias[e].reshape(-1, 1),
        alog=ssd_A_log[e].reshape(1, -1), alogT=ssd_A_log[e].reshape(-1, 1),
        dexp=jnp.repeat(ssd_D[e], SSD_HEAD_DIM)[None, :],
        snw=ssd_norm_w[e][None, :],
        lq=diff_lambda[e], sw=diff_subln_w[e][None, :],
        wo=w_out_ab[e].astype(BF16))


def _prep_odd(o, w_in_c, s5_lambda_re, s5_lambda_im, s5_log_dt, s5_B_re, s5_B_im, s5_C_re,
              s5_C_im, s5_D, w_glu, b_glu, w_out_c):
    w = w_in_c[o]
    w1, qm, a1, a2 = _s5_weights(s5_lambda_re[o], s5_lambda_im[o], s5_log_dt[o], s5_B_re[o],
                                 s5_B_im[o], s5_C_re[o], s5_C_im[o])
    dexp = jnp.broadcast_to(s5_D[o].reshape(S5_GROUPS, S5_GROUP, 1), (S5_GROUPS, S5_GROUP, S5_T))
    return dict(wuT=w[:, :D_MODEL].T.astype(BF16), wz=w[:, D_MODEL:].astype(BF16),
                w1=w1, qm=qm, a1=a1, a2=a2, dexp=dexp,
                wg=w_glu[o].astype(BF16), bg=b_glu[o][None, :], wo=w_out_c[o].astype(BF16))


def _even_layer(x2d, nw, p, bt, *, nb, L, lambda_init):
    z, xbc, q, k, v, g, dt, dtT = _proj_even(x2d, nw, p["wm"], p["wdt"], p["wdtT"])
    y_ssd = _ssd(z, xbc, dt, dtT, p["cw"], p["cb"], p["dtb"], p["dtbT"], p["alog"], p["alogT"],
                 p["dexp"], p["snw"], nb=nb, L=L)
    y_att = _attn(q, k, v, g, bt, p["lq"], p["sw"], nb=nb, L=L, lambda_init=lambda_init)
    return _out_even(y_ssd, y_att, p["wo"], x2d)


def _odd_layer(x2d, nw, p, fw, *, nb, L, final):
    T = nb * L
    nchunk = L // S5_T
    uT, zc = _proj_odd(x2d, nw, p["wuT"], p["wz"])
    uT3 = uT.reshape(D_MODEL, T // S5_T, S5_T)
    yT3 = _s5(uT3, p["w1"], p["qm"], p["a1"], p["a2"], p["dexp"], nchunk=nchunk)
    y = jnp.transpose(yT3, (1, 2, 0)).reshape(T, D_MODEL)
    return _out_odd(y, zc, x2d, p["wg"], p["bg"], p["wo"], fw, final=final)


def kernel(x_prompt, x_sample, norm_w, final_norm_w, rel_bias, w_in_ab, conv_w, conv_b,
           ssd_dt_bias, ssd_A_log, ssd_D, ssd_norm_w, diff_lambda, diff_subln_w, w_out_ab,
           w_in_c, s5_lambda_re, s5_lambda_im, s5_log_dt, s5_B_re, s5_B_im, s5_C_re, s5_C_im,
           s5_D, w_glu, b_glu, w_out_c):
    even = [_prep_even(e, w_in_ab, conv_w, conv_b, ssd_dt_bias, ssd_A_log, ssd_D, ssd_norm_w,
                       diff_lambda, diff_subln_w, w_out_ab) for e in range((DEPTH + 1) // 2)]
    odd = [_prep_odd(o, w_in_c, s5_lambda_re, s5_lambda_im, s5_log_dt, s5_B_re, s5_B_im,
                     s5_C_re, s5_C_im, s5_D, w_glu, b_glu, w_out_c) for o in range(DEPTH // 2)]
    fw = final_norm_w[None, :]

    def run(x):
        nb, L, _ = x.shape
        bt = _bias_tiles(rel_bias, L)
        x2d = x.reshape(nb * L, D_MODEL)
        for l in range(DEPTH):
            nw = norm_w[l][None, :]
            if l % 2 == 0:
                lambda_init = 0.8 - 0.6 * math.exp(-0.3 * l)
                x2d = _even_layer(x2d, nw, even[l // 2], bt, nb=nb, L=L, lambda_init=lambda_init)
            else:
                x2d = _odd_layer(x2d, nw, odd[l // 2], fw, nb=nb, L=L, final=(l == DEPTH - 1))
        return x2d.reshape(nb, L, D_MODEL)

    return (run(x_prompt), run(x_sample))
```

```python
import functools
import math

import jax
import jax.numpy as jnp
from jax import lax
from jax.experimental import pallas as pl
from jax.experimental.pallas import tpu as pltpu

F32 = jnp.float32
BF16 = jnp.bfloat16

D_MODEL = 1024
DEPTH = 4
EPS = 1e-5

SSD_HEADS = 16
SSD_HEAD_DIM = 64
SSD_GROUPS = 2
SSD_STATE = 64
D_CONV = 5
CONV_CH = D_MODEL + 2 * SSD_GROUPS * SSD_STATE
SSD_Q = 128
CONV_HALO = 16

DA_HEADS = 8
DA_HEAD_DIM = 64
N_BUCKETS = 32
MAX_DISTANCE = 128
ATT_QB = 256
BIAS_TILE = 128

S5_GROUP = 16
S5_GROUPS = 64
S5_STATE = 64
S5_T = 128

TM_MAX = 512
VMEM_LIMIT = 56 * 1024 * 1024


def _cparams(sem):
    return pltpu.CompilerParams(dimension_semantics=sem, vmem_limit_bytes=VMEM_LIMIT)


def _silu(x):
    return x * jax.nn.sigmoid(x)


def _rms_rows(x, w):
    ms = jnp.mean(x * x, axis=-1, keepdims=True)
    return x * lax.rsqrt(ms + EPS) * w


EVEN_SPLITS = (("z", 0, 1024), ("xbc", 1024, 1280), ("q", 2304, 1024), ("k", 3328, 1024),
               ("v", 4352, 1024), ("g", 5376, 1024))
EVEN_MAIN = 6400


def _proj_even_kernel(x_ref, nw_ref, wm_ref, wdt_ref, wdtT_ref,
                      z_ref, xbc_ref, q_ref, k_ref, v_ref, g_ref, dt_ref, dtT_ref):
    h = _rms_rows(x_ref[...], nw_ref[...]).astype(BF16)
    outs = (z_ref, xbc_ref, q_ref, k_ref, v_ref, g_ref)
    for ref, (_, off, n) in zip(outs, EVEN_SPLITS):
        ref[...] = jnp.dot(h, wm_ref[:, off:off + n], preferred_element_type=F32).astype(BF16)
    dt_ref[...] = jnp.dot(h, wdt_ref[...], preferred_element_type=F32)
    dtT_ref[...] = lax.dot_general(wdtT_ref[...], h, (((1,), (1,)), ((), ())),
                                   preferred_element_type=F32)


def _proj_even(x2d, nw, wm, wdt, wdtT):
    T = x2d.shape[0]
    TM = min(TM_MAX, T)
    row = lambda n: pl.BlockSpec((TM, n), lambda i: (i, 0))
    full = lambda a: pl.BlockSpec(a.shape, lambda i: (0,) * a.ndim)
    out_shape = [jax.ShapeDtypeStruct((T, n), BF16) for (_, _, n) in EVEN_SPLITS]
    out_shape += [jax.ShapeDtypeStruct((T, 32), F32), jax.ShapeDtypeStruct((32, T), F32)]
    out_specs = [row(n) for (_, _, n) in EVEN_SPLITS]
    out_specs += [row(32), pl.BlockSpec((32, TM), lambda i: (0, i))]
    return pl.pallas_call(
        _proj_even_kernel,
        grid=(T // TM,),
        in_specs=[row(D_MODEL), full(nw), full(wm), full(wdt), full(wdtT)],
        out_specs=out_specs,
        out_shape=out_shape,
        compiler_params=_cparams(("parallel",)),
        name="proj_even",
    )(x2d, nw, wm, wdt, wdtT)


def _ssd_kernel(z_ref, xbc_ref, dt_ref, dtT_ref, cw_ref, cb_ref, dtb_ref, dtbT_ref,
                alog_ref, alogT_ref, dexp_ref, nw_ref, o_ref,
                xpad, xconv, hb_all, hf_st, hb_st, sel128, sel64, *, L):
    Q = SSD_Q
    H = SSD_HEADS
    nc = L // Q
    NP = H // 2

    zeros_h = jnp.zeros((CONV_HALO, CONV_CH), BF16)
    xpad[0:CONV_HALO, :] = zeros_h
    xpad[L + CONV_HALO:L + 2 * CONV_HALO, :] = zeros_h

    def stage_body(c, carry):
        t0 = pl.multiple_of(c * Q, Q)
        xpad[pl.ds(t0 + CONV_HALO, Q), :] = xbc_ref[pl.ds(t0, Q), :]
        return carry
    lax.fori_loop(0, nc, stage_body, 0)

    def conv_body(c, carry):
        t0 = pl.multiple_of(c * Q, Q)
        slab = xpad[pl.ds(t0, Q + 2 * CONV_HALO), :].astype(F32)
        acc = jnp.zeros((Q, CONV_CH), F32) + cb_ref[...]
        for kk in range(D_CONV):
            sh = (D_CONV // 2 - kk) % (Q + 2 * CONV_HALO)
            r = slab if sh == 0 else pltpu.roll(slab, sh, axis=0)
            acc = acc + r[CONV_HALO:CONV_HALO + Q, :] * cw_ref[kk:kk + 1, :]
        xconv[pl.ds(t0, Q), :] = _silu(acc).astype(BF16)
        return carry
    lax.fori_loop(0, nc, conv_body, 0)

    ri = lax.broadcasted_iota(jnp.int32, (Q, Q), 0)
    ci = lax.broadcasted_iota(jnp.int32, (Q, Q), 1)
    lower_incl = (ri >= ci).astype(F32)
    upper_incl = (ci >= ri).astype(F32)
    strict_lower = ri > ci
    eye = ri == ci
    lane128 = lax.broadcasted_iota(jnp.int32, (Q, 128), 1)
    lo_half = lane128 < 64
    col32 = lax.broadcasted_iota(jnp.int32, (Q, 2 * H), 1)
    row32 = lax.broadcasted_iota(jnp.int32, (2 * H, Q), 0)
    A_col = -jnp.exp(alog_ref[...])
    A_row = -jnp.exp(alogT_ref[...])
    hi = lax.Precision.HIGHEST

    def chunk_scalars(t0):
        dt_c = jax.nn.softplus(dt_ref[pl.ds(t0, Q), :] + dtb_ref[...])
        a_c = dt_c * A_col
        pre_c = jnp.dot(lower_incl, a_c, precision=hi, preferred_element_type=F32)
        suf_c = jnp.dot(upper_incl, a_c, precision=hi, preferred_element_type=F32)
        cs_c = jnp.where(col32 < H, pre_c, suf_c)
        dt_r = jax.nn.softplus(dtT_ref[:, pl.ds(t0, Q)] + dtbT_ref[...])
        a_r = dt_r * A_row
        pre_r = jnp.dot(a_r, upper_incl, precision=hi, preferred_element_type=F32)
        suf_r = jnp.dot(a_r, lower_incl, precision=hi, preferred_element_type=F32)
        cs_r = jnp.where(row32 < H, pre_r, suf_r)
        tot = jnp.where(col32[0:1, :] < H, cs_c[Q - 1:Q, :], cs_c[0:1, :])
        return dt_c, cs_c, dt_r, cs_r, tot

    for sel_ref, width in ((sel128, 128), (sel64, 64)):
        n = 2 * H * width
        src = lax.broadcasted_iota(jnp.int32, (3 * 2 * H, n), 0) % (2 * H)
        dst = lax.broadcasted_iota(jnp.int32, (3 * 2 * H, n), 1) // width
        sel_ref[...] = jnp.where(src == dst, 1.0, 0.0).astype(BF16)

    def replicate(small, sel_ref, lo, hi_col):
        width = sel_ref.shape[1] // (2 * H)
        p1 = small.astype(BF16)
        r1 = small - p1.astype(F32)
        p2 = r1.astype(BF16)
        p3 = (r1 - p2.astype(F32)).astype(BF16)
        pieces = jnp.concatenate([p1, p2, p3], axis=1)
        return jnp.dot(pieces, sel_ref[:, lo * width:hi_col * width],
                       preferred_element_type=F32)

    def group_mask(x128, g):
        keep = lo_half if g == 0 else jnp.logical_not(lo_half)
        return jnp.where(keep, x128, jnp.zeros_like(x128))

    def state_update(st_ref, Bm, xs_scaled, decay_row):
        for g in range(SSD_GROUPS):
            Bg = group_mask(Bm, g)
            dS = lax.dot_general(Bg, xs_scaled[:, g * 512:(g + 1) * 512],
                                 (((0,), (0,)), ((), ())), preferred_element_type=F32)
            st_ref[g] = st_ref[g] * decay_row[:, g * 512:(g + 1) * 512] + dS

    hb_st[...] = jnp.zeros_like(hb_st)

    def bwd_body(cc, carry):
        c = nc - 1 - cc
        t0 = pl.multiple_of(c * Q, Q)
        dt_c, cs_c, _, _, tot = chunk_scalars(t0)
        hb_all[c] = hb_st[...].astype(BF16)
        w_state = jnp.exp(tot - cs_c) * dt_c
        e4 = replicate(w_state, sel64, H, 2 * H)
        decay = replicate(jnp.exp(tot), sel64, H, 2 * H)
        xs = xconv[pl.ds(t0, Q), 0:D_MODEL].astype(F32)
        Bm = xconv[pl.ds(t0, Q), D_MODEL:D_MODEL + 128]
        state_update(hb_st, Bm, (xs * e4).astype(BF16), decay)
        return carry
    lax.fori_loop(0, nc, bwd_body, 0)

    hf_st[...] = jnp.zeros_like(hf_st)

    def fwd_body(c, carry):
        t0 = pl.multiple_of(c * Q, Q)
        dt_c, cs_c, dt_r, cs_r, tot = chunk_scalars(t0)
        xs_b = xconv[pl.ds(t0, Q), 0:D_MODEL]
        xs = xs_b.astype(F32)
        Bm = xconv[pl.ds(t0, Q), D_MODEL:D_MODEL + 128]
        Cm = xconv[pl.ds(t0, Q), D_MODEL + 128:D_MODEL + 256]
        ecs = jnp.exp(cs_c)
        e12 = replicate(ecs, sel64, 0, 2 * H)
        e1 = e12[:, :D_MODEL]
        e2 = e12[:, D_MODEL:]
        colrep = replicate(cs_c, sel128, 0, 2 * H)
        rterm = jnp.log(dt_r) - cs_r

        y_parts = []
        for g in range(SSD_GROUPS):
            Cg = group_mask(Cm, g)
            cbm = lax.dot_general(Cg, Bm, (((1,), (1,)), ((), ())),
                                  preferred_element_type=F32)
            yoff_f = jnp.dot(Cg, hf_st[g].astype(BF16), preferred_element_type=F32)
            yoff_b = jnp.dot(Cg, hb_all[c, g], preferred_element_type=F32)
            sl = slice(g * 512, (g + 1) * 512)
            ydiag = []
            for pp in range(NP // SSD_GROUPS):
                mats = []
                for hh in range(2):
                    h = g * (H // SSD_GROUPS) + 2 * pp + hh
                    ef = colrep[:, h * Q:(h + 1) * Q] + rterm[h:h + 1, :]
                    eb = colrep[:, (H + h) * Q:(H + h + 1) * Q] + rterm[H + h:H + h + 1, :]
                    w = jnp.exp(jnp.where(strict_lower, ef, eb))
                    w = w + jnp.where(eye, dt_r[h:h + 1, :], 0.0)
                    mats.append((cbm * w).astype(BF16))
                lhs = jnp.concatenate(mats, axis=1)
                p_abs = g * (NP // SSD_GROUPS) + pp
                xp = xs_b[:, p_abs * 128:(p_abs + 1) * 128]
                rhs = jnp.concatenate(
                    [jnp.where(lo_half, xp, jnp.zeros_like(xp)),
                     jnp.where(lo_half, jnp.zeros_like(xp), xp)], axis=0)
                ydiag.append(jnp.dot(lhs, rhs, preferred_element_type=F32))
            ydiag = jnp.concatenate(ydiag, axis=1)
            y_parts.append(ydiag + yoff_f * e1[:, sl] + yoff_b * e2[:, sl])
        y = xs * dexp_ref[...] + jnp.concatenate(y_parts, axis=1)

        w_state = jnp.exp(tot - cs_c) * dt_c
        e3 = replicate(w_state, sel64, 0, H)
        state_update(hf_st, Bm, (xs * e3).astype(BF16), e1[Q - 1:Q, :])

        zg = z_ref[pl.ds(t0, Q), :].astype(F32)
        o_ref[pl.ds(t0, Q), :] = _rms_rows(y * _silu(zg), nw_ref[...]).astype(BF16)
        return carry
    lax.fori_loop(0, nc, fwd_body, 0)


def _ssd(z, xbc, dt, dtT, cw, cb, dtb, dtbT, alog, alogT, dexp, nw, *, nb, L):
    full = lambda a: pl.BlockSpec(a.shape, lambda b: (0,) * a.ndim)
    nc = L // SSD_Q
    return pl.pallas_call(
        functools.partial(_ssd_kernel, L=L),
        grid=(nb,),
        in_specs=[pl.BlockSpec((L, D_MODEL), lambda b: (b, 0)),
                  pl.BlockSpec((L, CONV_CH), lambda b: (b, 0)),
                  pl.BlockSpec((L, 32), lambda b: (b, 0)),
                  pl.BlockSpec((32, L), lambda b: (0, b)),
                  full(cw), full(cb), full(dtb), full(dtbT), full(alog), full(alogT),
                  full(dexp), full(nw)],
        out_specs=pl.BlockSpec((L, D_MODEL), lambda b: (b, 0)),
        out_shape=jax.ShapeDtypeStruct((nb * L, D_MODEL), BF16),
        scratch_shapes=[pltpu.VMEM((L + 2 * CONV_HALO, CONV_CH), BF16),
                        pltpu.VMEM((L, CONV_CH), BF16),
                        pltpu.VMEM((nc, SSD_GROUPS, 128, 512), BF16),
                        pltpu.VMEM((SSD_GROUPS, 128, 512), F32),
                        pltpu.VMEM((SSD_GROUPS, 128, 512), F32),
                        pltpu.VMEM((3 * 2 * SSD_HEADS, 2 * SSD_HEADS * 128), BF16),
                        pltpu.VMEM((3 * 2 * SSD_HEADS, 2 * SSD_HEADS * 64), BF16)],
        compiler_params=_cparams(("parallel",)),
        name="ssd_mixer",
    )(z, xbc, dt, dtT, cw, cb, dtb, dtbT, alog, alogT, dexp, nw)


def _attn_kernel(q_ref, k_ref, v_ref, g_ref, bt_ref, lq_ref, sw_ref, o_ref, *, L, lambda_init):
    QB = ATT_QB
    nbt = L // BIAS_TILE
    k = k_ref[...]
    v = v_ref[...]
    v_ext = jnp.concatenate([v, jnp.ones_like(v)], axis=1)
    lq = lq_ref[...]
    lam = (jnp.exp(jnp.sum(lq[0:1] * lq[1:2], axis=-1, keepdims=True))
           - jnp.exp(jnp.sum(lq[2:3] * lq[3:4], axis=-1, keepdims=True)) + lambda_init)
    lane = lax.broadcasted_iota(jnp.int32, (QB, 128), 1)
    lo_half = lane < 64
    scale = DA_HEAD_DIM ** -0.5

    for qi in range(L // QB):
        q = q_ref[qi * QB:(qi + 1) * QB, :] * scale
        bias = jnp.concatenate(
            [bt_ref[0, :, (nbt - 1 - (qi * (QB // BIAS_TILE) + r)) * BIAS_TILE:
                    (nbt - 1 - (qi * (QB // BIAS_TILE) + r)) * BIAS_TILE + L]
             for r in range(QB // BIAS_TILE)], axis=0)
        outs = []
        for c in range(2):
            qc = jnp.where(lo_half if c == 0 else jnp.logical_not(lo_half), q, jnp.zeros_like(q))
            s = lax.dot_general(qc, k, (((1,), (1,)), ((), ())), preferred_element_type=F32)
            s = s + bias
            m = jnp.max(s, axis=-1, keepdims=True)
            e = jnp.exp(s - m).astype(BF16)
            pv = jnp.dot(e, v_ext, preferred_element_type=F32)
            outs.append(pv[:, 0:128] / pv[:, 128:256])
        o = outs[0] - lam * outs[1]
        o = _rms_rows(o, sw_ref[...]) * (1.0 - lambda_init)
        gg = g_ref[qi * QB:(qi + 1) * QB, :].astype(F32)
        o_ref[qi * QB:(qi + 1) * QB, :] = (o * _silu(gg)).astype(BF16)


def _attn(q, k, v, g, bt, lq, sw, *, nb, L, lambda_init):
    blk = pl.BlockSpec((L, 128), lambda b, h: (b, h))
    return pl.pallas_call(
        functools.partial(_attn_kernel, L=L, lambda_init=lambda_init),
        grid=(nb, DA_HEADS),
        in_specs=[blk, blk, blk, blk,
                  pl.BlockSpec((1,) + bt.shape[1:], lambda b, h: (h, 0, 0)),
                  pl.BlockSpec(lq.shape, lambda b, h: (0, 0)),
                  pl.BlockSpec(sw.shape, lambda b, h: (0, 0))],
        out_specs=blk,
        out_shape=jax.ShapeDtypeStruct((nb * L, D_MODEL), BF16),
        compiler_params=_cparams(("parallel", "parallel")),
        name="diff_attn",
    )(q, k, v, g, bt, lq, sw)


def _t5_bucket(rel):
    half = N_BUCKETS // 2
    max_exact = half // 2
    n = jnp.abs(rel)
    large = max_exact + (jnp.log(jnp.maximum(n, 1).astype(F32) / max_exact)
                         / math.log(MAX_DISTANCE / max_exact) * (half - max_exact)).astype(jnp.int32)
    large = jnp.minimum(large, half - 1)
    return jnp.where(rel > 0, half, 0) + jnp.where(n < max_exact, n, large)


def _bias_tiles(rel_bias, L):
    nbt = L // BIAS_TILE
    r = jnp.arange(BIAS_TILE)[:, None]
    c = jnp.arange((2 * nbt - 1) * BIAS_TILE)[None, :]
    bucket = _t5_bucket(c - r - (nbt - 1) * BIAS_TILE)[None]
    table = rel_bias.astype(F32).T[:, :, None, None]
    out = jnp.zeros((DA_HEADS,) + bucket.shape[1:], F32)
    for b in range(N_BUCKETS):
        out = jnp.where(bucket == b, table[:, b], out)
    return out


def _out_even_kernel(a_ref, b_ref, w_ref, x_ref, o_ref):
    acc = jnp.dot(a_ref[...], w_ref[0:D_MODEL, :], preferred_element_type=F32)
    acc = acc + jnp.dot(b_ref[...], w_ref[D_MODEL:2 * D_MODEL, :], preferred_element_type=F32)
    o_ref[...] = x_ref[...] + acc


def _out_even(a, b, w, x2d):
    T = x2d.shape[0]
    TM = min(TM_MAX, T)
    row = pl.BlockSpec((TM, D_MODEL), lambda i: (i, 0))
    return pl.pallas_call(
        _out_even_kernel,
        grid=(T // TM,),
        in_specs=[row, row, pl.BlockSpec(w.shape, lambda i: (0, 0)), row],
        out_specs=row,
        out_shape=jax.ShapeDtypeStruct((T, D_MODEL), F32),
        compiler_params=_cparams(("parallel",)),
        name="out_even",
    )(a, b, w, x2d)


def _proj_odd_kernel(x_ref, nw_ref, wuT_ref, wz_ref, uT_ref, zc_ref):
    h = _rms_rows(x_ref[...], nw_ref[...]).astype(BF16)
    uT_ref[...] = lax.dot_general(wuT_ref[...], h, (((1,), (1,)), ((), ())),
                                  preferred_element_type=F32).astype(BF16)
    zc_ref[...] = jnp.dot(h, wz_ref[...], preferred_element_type=F32).astype(BF16)


def _proj_odd(x2d, nw, wuT, wz):
    T = x2d.shape[0]
    TM = min(TM_MAX, T)
    row = pl.BlockSpec((TM, D_MODEL), lambda i: (i, 0))
    full = lambda a: pl.BlockSpec(a.shape, lambda i: (0,) * a.ndim)
    return pl.pallas_call(
        _proj_odd_kernel,
        grid=(T // TM,),
        in_specs=[row, full(nw), full(wuT), full(wz)],
        out_specs=[pl.BlockSpec((D_MODEL, TM), lambda i: (0, i)), row],
        out_shape=[jax.ShapeDtypeStruct((D_MODEL, T), BF16),
                   jax.ShapeDtypeStruct((T, D_MODEL), BF16)],
        compiler_params=_cparams(("parallel",)),
        name="proj_odd",
    )(x2d, nw, wuT, wz)


def _s5_kernel(xp_ref, xs_ref, lag_ref, pm_ref, q_ref, a1_ref, a2_ref, d_ref, op_ref, os_ref,
               m_scr, s_scr, h_scr, *, nchunk):
    GS = S5_GROUP
    T = S5_T
    Rp = xp_ref.shape[1]
    R = Rp + xs_ref.shape[1]
    nseq = R // nchunk

    def build(ci, carry):
        r0 = pl.multiple_of(ci * T, T)
        for co in range(GS):
            w = jnp.broadcast_to(lag_ref[0, ci, co:co + 1, :], (T, 2 * T))
            w = pltpu.roll(w, 1, 1, stride=1, stride_axis=0)
            m_scr[pl.ds(r0, T), co * T:(co + 1) * T] = w[:, T:].astype(BF16)
        return carry
    lax.fori_loop(0, GS, build, 0)

    lhs = jnp.concatenate([jnp.concatenate([xp_ref[c], xs_ref[c]], axis=0) for c in range(GS)],
                          axis=1)
    st = jnp.dot(lhs, pm_ref[0], preferred_element_type=F32)
    s_scr[0] = st[:, 0:128]
    s_scr[1] = st[:, 128:256]
    a1 = a1_ref[0]
    a2 = a2_ref[0]

    def run(d, order):
        a1d, a2d = a1[:, d * 128:(d + 1) * 128], a2[:, d * 128:(d + 1) * 128]
        h = jnp.zeros((nseq, 128), F32)
        for kk in order:
            rows = pl.ds(kk, nseq, stride=nchunk)
            h_scr[d, rows, :] = h
            h = h * a1d + pltpu.roll(h, 64, axis=1) * a2d + s_scr[d, rows, :]
    run(0, range(nchunk))
    run(1, range(nchunk - 1, -1, -1))

    hprev = jnp.concatenate([h_scr[0], h_scr[1]], axis=1).astype(BF16)
    y = (jnp.dot(lhs, m_scr[...], preferred_element_type=F32)
         + jnp.dot(hprev, q_ref[0], preferred_element_type=F32))
    for c in range(GS):
        yc = y[:, c * T:(c + 1) * T]
        dc = d_ref[0, c:c + 1, :]
        op_ref[c] = (yc[:Rp] + dc * xp_ref[c].astype(F32)).astype(BF16)
        os_ref[c] = (yc[Rp:] + dc * xs_ref[c].astype(F32)).astype(BF16)


def _s5(uT3p, uT3s, lag, pm, qm, a1, a2, dexp, *, nchunk):
    Rp, Rs = uT3p.shape[1], uT3s.shape[1]
    xblk = lambda r: pl.BlockSpec((S5_GROUP, r, S5_T), lambda g: (g, 0, 0))
    per_g = lambda a: pl.BlockSpec((1,) + a.shape[1:], lambda g: (g,) + (0,) * (a.ndim - 1))
    KT = S5_GROUP * S5_T
    return pl.pallas_call(
        functools.partial(_s5_kernel, nchunk=nchunk),
        grid=(S5_GROUPS,),
        in_specs=[xblk(Rp), xblk(Rs), per_g(lag), per_g(pm), per_g(qm), per_g(a1), per_g(a2),
                  per_g(dexp)],
        out_specs=[xblk(Rp), xblk(Rs)],
        out_shape=[jax.ShapeDtypeStruct(uT3p.shape, BF16), jax.ShapeDtypeStruct(uT3s.shape, BF16)],
        scratch_shapes=[pltpu.VMEM((KT, KT), BF16),
                        pltpu.VMEM((2, Rp + Rs, 128), F32), pltpu.VMEM((2, Rp + Rs, 128), F32)],
        compiler_params=_cparams(("parallel",)),
        name="s5_mixer",
    )(uT3p, uT3s, lag, pm, qm, a1, a2, dexp)


def _s5_weights(lam_re, lam_im, log_dt, B_re, B_im, C_re, C_im):
    T = S5_T
    hp = lax.Precision.HIGHEST
    step = jnp.exp(log_dt)[:, :, None]
    lr, li = lam_re, lam_im
    t = jnp.arange(T + 1, dtype=F32)[:, None, None, None]
    pw_mag = jnp.exp(lr * step * t)
    pw_re = pw_mag * jnp.cos(li * step * t)
    pw_im = pw_mag * jnp.sin(li * step * t)
    ab_re, ab_im = pw_re[1], pw_im[1]
    den = lr * lr + li * li
    nr = ab_re - 1.0
    cr = (nr * lr + ab_im * li) / den
    ci = (ab_im * lr - nr * li) / den
    bb_re = cr[..., None] * B_re - ci[..., None] * B_im
    bb_im = cr[..., None] * B_im + ci[..., None] * B_re
    ca_re = C_re[None] * pw_re[:, :, :, None, :] - C_im[None] * pw_im[:, :, :, None, :]
    ca_im = C_re[None] * pw_im[:, :, :, None, :] + C_im[None] * pw_re[:, :, :, None, :]
    kk = (jnp.einsum('tdgcp,dgpk->tdgck', ca_re[:T], bb_re, precision=hp)
          - jnp.einsum('tdgcp,dgpk->tdgck', ca_im[:T], bb_im, precision=hp))
    kf, kb = kk[:, 0], kk[:, 1]
    lagged = jnp.concatenate([kb[:0:-1], (kf[0] + kb[0])[None], kf[1:]], axis=0)
    lag = jnp.pad(jnp.transpose(lagged, (1, 3, 2, 0)), ((0, 0), (0, 0), (0, 0), (0, 1)))
    G = lag.shape[0]
    pf_re, pf_im = pw_re[:T][::-1, 0], pw_im[:T][::-1, 0]
    pb_re, pb_im = pw_re[:T, 1], pw_im[:T, 1]

    def state_proj(p_re, p_im, d):
        s_re = p_re[:, :, :, None] * bb_re[d][None] - p_im[:, :, :, None] * bb_im[d][None]
        s_im = p_re[:, :, :, None] * bb_im[d][None] + p_im[:, :, :, None] * bb_re[d][None]
        s = jnp.concatenate([s_re, s_im], axis=2)
        return jnp.transpose(s, (1, 3, 0, 2))
    pm = jnp.concatenate([state_proj(pf_re, pf_im, 0), state_proj(pb_re, pb_im, 1)], axis=-1)
    pm = pm.reshape(G, S5_GROUP * T, 4 * S5_STATE).astype(BF16)
    def out_proj(re, im):
        o = jnp.concatenate([re, -im], axis=-1)
        return jnp.transpose(o, (1, 3, 2, 0))
    qf = out_proj(ca_re[1:T + 1, 0], ca_im[1:T + 1, 0])
    qb = out_proj(ca_re[T:0:-1, 1], ca_im[T:0:-1, 1])
    qm = jnp.concatenate([qf, qb], axis=1).reshape(G, 4 * S5_STATE, S5_GROUP * T).astype(BF16)
    at_re, at_im = pw_re[T], pw_im[T]
    a1 = jnp.concatenate([at_re[0], at_re[0], at_re[1], at_re[1]], axis=-1)[:, None, :]
    a2 = jnp.concatenate([-at_im[0], at_im[0], -at_im[1], at_im[1]], axis=-1)[:, None, :]
    return lag, pm, qm, a1, a2


def _out_odd_kernel(y_ref, zc_ref, x_ref, wg_ref, bg_ref, wo_ref, fw_ref, o_ref, *, final):
    h = jax.nn.gelu(y_ref[...].astype(F32))
    gate = jax.nn.sigmoid(jnp.dot(h.astype(BF16), wg_ref[...], preferred_element_type=F32)
                          + bg_ref[...])
    t = h * gate * _silu(zc_ref[...].astype(F32))
    xn = x_ref[...] + jnp.dot(t.astype(BF16), wo_ref[...], preferred_element_type=F32)
    o_ref[...] = _rms_rows(xn, fw_ref[...]) if final else xn


def _out_odd(y, zc, x2d, wg, bg, wo, fw, *, final):
    T = x2d.shape[0]
    TM = min(TM_MAX, T)
    row = pl.BlockSpec((TM, D_MODEL), lambda i: (i, 0))
    full = lambda a: pl.BlockSpec(a.shape, lambda i: (0,) * a.ndim)
    return pl.pallas_call(
        functools.partial(_out_odd_kernel, final=final),
        grid=(T // TM,),
        in_specs=[row, row, row, full(wg), full(bg), full(wo), full(fw)],
        out_specs=row,
        out_shape=jax.ShapeDtypeStruct((T, D_MODEL), F32),
        compiler_params=_cparams(("parallel",)),
        name="out_odd",
    )(y, zc, x2d, wg, bg, wo, fw)


def _prep_even(e, w_in_ab, conv_w, conv_b, ssd_dt_bias, ssd_A_log, ssd_D, ssd_norm_w,
               diff_lambda, diff_subln_w, w_out_ab):
    w = w_in_ab[e]
    o_dt = D_MODEL + CONV_CH
    wm = jnp.concatenate([w[:, :o_dt], w[:, o_dt + 2 * SSD_HEADS:]], axis=1).astype(BF16)
    wdt = w[:, o_dt:o_dt + 2 * SSD_HEADS].astype(BF16)
    return dict(
        wm=wm, wdt=wdt, wdtT=wdt.T,
        cw=conv_w[e], cb=conv_b[e][None, :],
        dtb=ssd_dt_bias[e].reshape(1, -1), dtbT=ssd_dt_bias[e].reshape(-1, 1),
        alog=ssd_A_log[e].reshape(1, -1), alogT=ssd_A_log[e].reshape(-1, 1),
        dexp=jnp.repeat(ssd_D[e], SSD_HEAD_DIM)[None, :],
        snw=ssd_norm_w[e][None, :],
        lq=diff_lambda[e], sw=diff_subln_w[e][None, :],
        wo=w_out_ab[e].astype(BF16))


def _prep_odd(o, w_in_c, s5_lambda_re, s5_lambda_im, s5_log_dt, s5_B_re, s5_B_im, s5_C_re,
              s5_C_im, s5_D, w_glu, b_glu, w_out_c):
    w = w_in_c[o]
    lag, pm, qm, a1, a2 = _s5_weights(s5_lambda_re[o], s5_lambda_im[o], s5_log_dt[o], s5_B_re[o],
                                      s5_B_im[o], s5_C_re[o], s5_C_im[o])
    dexp = jnp.broadcast_to(s5_D[o].reshape(S5_GROUPS, S5_GROUP, 1), (S5_GROUPS, S5_GROUP, S5_T))
    return dict(wuT=w[:, :D_MODEL].T.astype(BF16), wz=w[:, D_MODEL:].astype(BF16),
                lag=lag, pm=pm, qm=qm, a1=a1, a2=a2, dexp=dexp,
                wg=w_glu[o].astype(BF16), bg=b_glu[o][None, :], wo=w_out_c[o].astype(BF16))


def _even_layer(x2d, nw, p, bt, *, nb, L, lambda_init):
    z, xbc, q, k, v, g, dt, dtT = _proj_even(x2d, nw, p["wm"], p["wdt"], p["wdtT"])
    y_ssd = _ssd(z, xbc, dt, dtT, p["cw"], p["cb"], p["dtb"], p["dtbT"], p["alog"], p["alogT"],
                 p["dexp"], p["snw"], nb=nb, L=L)
    y_att = _attn(q, k, v, g, bt, p["lq"], p["sw"], nb=nb, L=L, lambda_init=lambda_init)
    return _out_even(y_ssd, y_att, p["wo"], x2d)


def _odd_layer(xs, nw, p, fw, *, L, final):
    proj = [_proj_odd(x2d, nw, p["wuT"], p["wz"]) for x2d in xs]
    uT3 = [uT.reshape(D_MODEL, uT.shape[1] // S5_T, S5_T) for uT, _ in proj]
    yT3 = _s5(uT3[0], uT3[1], p["lag"], p["pm"], p["qm"], p["a1"], p["a2"], p["dexp"],
              nchunk=L // S5_T)
    out = []
    for x2d, (_, zc), y3 in zip(xs, proj, yT3):
        y = jnp.transpose(y3, (1, 2, 0)).reshape(x2d.shape)
        out.append(_out_odd(y, zc, x2d, p["wg"], p["bg"], p["wo"], fw, final=final))
    return out


def kernel(x_prompt, x_sample, norm_w, final_norm_w, rel_bias, w_in_ab, conv_w, conv_b,
           ssd_dt_bias, ssd_A_log, ssd_D, ssd_norm_w, diff_lambda, diff_subln_w, w_out_ab,
           w_in_c, s5_lambda_re, s5_lambda_im, s5_log_dt, s5_B_re, s5_B_im, s5_C_re, s5_C_im,
           s5_D, w_glu, b_glu, w_out_c):
    even = [_prep_even(e, w_in_ab, conv_w, conv_b, ssd_dt_bias, ssd_A_log, ssd_D, ssd_norm_w,
                       diff_lambda, diff_subln_w, w_out_ab) for e in range((DEPTH + 1) // 2)]
    odd = [_prep_odd(o, w_in_c, s5_lambda_re, s5_lambda_im, s5_log_dt, s5_B_re, s5_B_im,
                     s5_C_re, s5_C_im, s5_D, w_glu, b_glu, w_out_c) for o in range(DEPTH // 2)]
    fw = final_norm_w[None, :]
    L = x_prompt.shape[1]
    assert x_sample.shape[1] == L
    bt = _bias_tiles(rel_bias, L)
    shapes = [x_prompt.shape, x_sample.shape]
    xs = [x.reshape(-1, D_MODEL) for x in (x_prompt, x_sample)]
    for l in range(DEPTH):
        nw = norm_w[l][None, :]
        if l % 2 == 0:
            lambda_init = 0.8 - 0.6 * math.exp(-0.3 * l)
            xs = [_even_layer(x2d, nw, even[l // 2], bt, nb=shp[0], L=L, lambda_init=lambda_init)
                  for x2d, shp in zip(xs, shapes)]
        else:
            xs = _odd_layer(xs, nw, odd[l // 2], fw, L=L, final=(l == DEPTH - 1))
    return tuple(x2d.reshape(shp) for x2d, shp in zip(xs, shapes))
```

```python
import functools
import math

import jax
import jax.numpy as jnp
from jax import lax
from jax.experimental import pallas as pl
from jax.experimental.pallas import tpu as pltpu

F32 = jnp.float32
BF16 = jnp.bfloat16

D_MODEL = 1024
DEPTH = 4
EPS = 1e-5

SSD_HEADS = 16
SSD_HEAD_DIM = 64
SSD_GROUPS = 2
SSD_STATE = 64
D_CONV = 5
CONV_CH = D_MODEL + 2 * SSD_GROUPS * SSD_STATE
SSD_Q = 128
CONV_HALO = 16

DA_HEADS = 8
DA_HEAD_DIM = 64
N_BUCKETS = 32
MAX_DISTANCE = 128
ATT_QB = 128
BIAS_TILE = 128

S5_GROUP = 16
S5_GROUPS = 64
S5_STATE = 64
S5_T = 128

TM_MAX = 512
VMEM_LIMIT = 56 * 1024 * 1024


def _cparams(sem):
    return pltpu.CompilerParams(dimension_semantics=sem, vmem_limit_bytes=VMEM_LIMIT)


def _silu(x):
    return x * jax.nn.sigmoid(x)


def _rms_rows(x, w):
    ms = jnp.mean(x * x, axis=-1, keepdims=True)
    return x * lax.rsqrt(ms + EPS) * w


EVEN_SPLITS = (("z", 0, 1024), ("xbc", 1024, 1280), ("q", 2304, 1024), ("k", 3328, 1024),
               ("v", 4352, 1024), ("g", 5376, 1024))
EVEN_MAIN = 6400


def _proj_even_kernel(x_ref, nw_ref, wm_ref, wdtT_ref,
                      z_ref, xbc_ref, q_ref, k_ref, v_ref, g_ref, dtc_ref):
    h = _rms_rows(x_ref[...], nw_ref[...]).astype(BF16)
    outs = (z_ref, xbc_ref, q_ref, k_ref, v_ref, g_ref)
    for ref, (_, off, n) in zip(outs, EVEN_SPLITS):
        ref[...] = jnp.dot(h, wm_ref[:, off:off + n], preferred_element_type=F32).astype(BF16)
    dtT = lax.dot_general(wdtT_ref[...], h, (((1,), (1,)), ((), ())),
                          preferred_element_type=F32)
    nh = dtT.shape[0]
    for j in range(dtT.shape[1] // SSD_Q):
        dtc_ref[j * nh:(j + 1) * nh, :] = dtT[:, j * SSD_Q:(j + 1) * SSD_Q]


def _proj_even(x2d, nw, wm, wdtT):
    T = x2d.shape[0]
    TM = min(TM_MAX, T)
    row = lambda n: pl.BlockSpec((TM, n), lambda i: (i, 0))
    full = lambda a: pl.BlockSpec(a.shape, lambda i: (0,) * a.ndim)
    out_shape = [jax.ShapeDtypeStruct((T, n), BF16) for (_, _, n) in EVEN_SPLITS]
    nh = wdtT.shape[0]
    out_shape += [jax.ShapeDtypeStruct((T // SSD_Q * nh, SSD_Q), F32)]
    out_specs = [row(n) for (_, _, n) in EVEN_SPLITS]
    out_specs += [pl.BlockSpec((TM // SSD_Q * nh, SSD_Q), lambda i: (i, 0))]
    return pl.pallas_call(
        _proj_even_kernel,
        grid=(T // TM,),
        in_specs=[row(D_MODEL), full(nw), full(wm), full(wdtT)],
        out_specs=out_specs,
        out_shape=out_shape,
        compiler_params=_cparams(("parallel",)),
        name="proj_even",
    )(x2d, nw, wm, wdtT)


def _ssd_kernel(z_ref, xbc_ref, dtc_ref, cw_ref, cb_ref, dtb_ref, alog_ref, dexp_ref, nw_ref,
                o_ref, xpad, xconv, hb_all, hf_st, hb_st, sel128, sel64,
                dt_s, cs_s, rt_s, ws_s, ec_s, shift_m, *, L):
    Q = SSD_Q
    H = SSD_HEADS
    nc = L // Q
    NP = H // 2

    zeros_h = jnp.zeros((CONV_HALO, CONV_CH), BF16)
    xpad[0:CONV_HALO, :] = zeros_h
    xpad[L + CONV_HALO:L + 2 * CONV_HALO, :] = zeros_h

    def stage_body(c, carry):
        t0 = pl.multiple_of(c * Q, Q)
        xpad[pl.ds(t0 + CONV_HALO, Q), :] = xbc_ref[pl.ds(t0, Q), :]
        return carry
    lax.fori_loop(0, nc, stage_body, 0)

    centre = D_CONV // 2
    taps = [kk for kk in range(D_CONV) if kk != centre]
    sr = lax.broadcasted_iota(jnp.int32, (Q, Q + 2 * CONV_HALO), 0)
    sc = lax.broadcasted_iota(jnp.int32, (Q, Q + 2 * CONV_HALO), 1)
    for idx, kk in enumerate(taps):
        shift_m[idx * Q:(idx + 1) * Q, :] = jnp.where(
            sc == sr + (CONV_HALO + kk - centre), 1.0, 0.0).astype(BF16)

    def conv_body(c, carry):
        t0 = pl.multiple_of(c * Q, Q)
        slab = xpad[pl.ds(t0, Q + 2 * CONV_HALO), :]
        shifted = jnp.dot(shift_m[...], slab, preferred_element_type=F32)
        acc = cb_ref[...] + slab[CONV_HALO:CONV_HALO + Q, :].astype(F32) * cw_ref[centre:centre + 1, :]
        for idx, kk in enumerate(taps):
            acc = acc + shifted[idx * Q:(idx + 1) * Q, :] * cw_ref[kk:kk + 1, :]
        xconv[pl.ds(t0, Q), :] = _silu(acc).astype(BF16)
        return carry
    lax.fori_loop(0, nc, conv_body, 0)

    ri = lax.broadcasted_iota(jnp.int32, (Q, Q), 0)
    ci = lax.broadcasted_iota(jnp.int32, (Q, Q), 1)
    lower_incl = (ri >= ci).astype(F32)
    upper_incl = (ci >= ri).astype(F32)
    strict_lower = ri > ci
    eye = ri == ci
    lane128 = lax.broadcasted_iota(jnp.int32, (Q, 128), 1)
    lo_half = lane128 < 64
    hi = lax.Precision.HIGHEST

    is_fwd = lax.broadcasted_iota(jnp.int32, (nc * 2 * H, Q), 0) % (2 * H) < H
    dt_all = jax.nn.softplus(dtc_ref[...] + dtb_ref[...])
    a_all = dt_all * -jnp.exp(alog_ref[...])
    pre = jnp.dot(a_all, upper_incl, precision=hi, preferred_element_type=F32)
    suf = jnp.dot(a_all, lower_incl, precision=hi, preferred_element_type=F32)
    cs_all = jnp.where(is_fwd, pre, suf)
    tot = jnp.where(is_fwd[:, 0:1], cs_all[:, Q - 1:Q], cs_all[:, 0:1])
    dt_s[...] = dt_all
    cs_s[...] = cs_all
    rt_s[...] = jnp.log(dt_all) - cs_all
    ws_s[...] = jnp.exp(tot - cs_all) * dt_all
    ec_s[...] = jnp.exp(cs_all)

    for sel_ref, width in ((sel128, 128), (sel64, 64)):
        n = 2 * H * width
        src = lax.broadcasted_iota(jnp.int32, (3 * 2 * H, n), 0) % (2 * H)
        dst = lax.broadcasted_iota(jnp.int32, (3 * 2 * H, n), 1) // width
        sel_ref[...] = jnp.where(src == dst, 1.0, 0.0).astype(BF16)

    def replicate(rows, sel_ref, lo, hi_row):
        width = sel_ref.shape[1] // (2 * H)
        p1 = rows.astype(BF16)
        r1 = rows - p1.astype(F32)
        p2 = r1.astype(BF16)
        p3 = (r1 - p2.astype(F32)).astype(BF16)
        pieces = jnp.concatenate([p1, p2, p3], axis=0)
        return lax.dot_general(pieces, sel_ref[:, lo * width:hi_row * width],
                               (((0,), (0,)), ((), ())), preferred_element_type=F32)

    def group_mask(x128, g):
        keep = lo_half if g == 0 else jnp.logical_not(lo_half)
        return jnp.where(keep, x128, jnp.zeros_like(x128))

    def state_update(st_ref, Bm, xs_scaled, decay_row):
        for g in range(SSD_GROUPS):
            Bg = group_mask(Bm, g)
            dS = lax.dot_general(Bg, xs_scaled[:, g * 512:(g + 1) * 512],
                                 (((0,), (0,)), ((), ())), preferred_element_type=F32)
            st_ref[g] = st_ref[g] * decay_row[:, g * 512:(g + 1) * 512] + dS

    hb_st[...] = jnp.zeros_like(hb_st)

    def bwd_body(cc, carry):
        c = nc - 1 - cc
        t0 = pl.multiple_of(c * Q, Q)
        hb_all[c] = hb_st[...].astype(BF16)
        hr = pl.ds(pl.multiple_of(c * 2 * H, 2 * H), 2 * H)
        e4 = replicate(ws_s[hr, :], sel64, H, 2 * H)
        decay = replicate(ec_s[hr, :], sel64, H, 2 * H)[0:1, :]
        xs = xconv[pl.ds(t0, Q), 0:D_MODEL].astype(F32)
        Bm = xconv[pl.ds(t0, Q), D_MODEL:D_MODEL + 128]
        state_update(hb_st, Bm, (xs * e4).astype(BF16), decay)
        return carry
    lax.fori_loop(0, nc, bwd_body, 0)

    hf_st[...] = jnp.zeros_like(hf_st)

    def fwd_body(c, carry):
        t0 = pl.multiple_of(c * Q, Q)
        hr = pl.ds(pl.multiple_of(c * 2 * H, 2 * H), 2 * H)
        dt_r = dt_s[hr, :]
        rterm = rt_s[hr, :]
        xs_b = xconv[pl.ds(t0, Q), 0:D_MODEL]
        xs = xs_b.astype(F32)
        Bm = xconv[pl.ds(t0, Q), D_MODEL:D_MODEL + 128]
        Cm = xconv[pl.ds(t0, Q), D_MODEL + 128:D_MODEL + 256]
        e12 = replicate(ec_s[hr, :], sel64, 0, 2 * H)
        e1 = e12[:, :D_MODEL]
        e2 = e12[:, D_MODEL:]
        colrep = replicate(cs_s[hr, :], sel128, 0, 2 * H)

        y_parts = []
        for g in range(SSD_GROUPS):
            Cg = group_mask(Cm, g)
            cbm = lax.dot_general(Cg, Bm, (((1,), (1,)), ((), ())),
                                  preferred_element_type=F32)
            yoff_f = jnp.dot(Cg, hf_st[g].astype(BF16), preferred_element_type=F32)
            yoff_b = jnp.dot(Cg, hb_all[c, g], preferred_element_type=F32)
            sl = slice(g * 512, (g + 1) * 512)
            ydiag = []
            for pp in range(NP // SSD_GROUPS):
                mats = []
                for hh in range(2):
                    h = g * (H // SSD_GROUPS) + 2 * pp + hh
                    ef = colrep[:, h * Q:(h + 1) * Q] + rterm[h:h + 1, :]
                    eb = colrep[:, (H + h) * Q:(H + h + 1) * Q] + rterm[H + h:H + h + 1, :]
                    w = jnp.exp(jnp.where(strict_lower, ef, eb))
                    w = w + jnp.where(eye, dt_r[h:h + 1, :], 0.0)
                    mats.append((cbm * w).astype(BF16))
                lhs = jnp.concatenate(mats, axis=1)
                p_abs = g * (NP // SSD_GROUPS) + pp
                xp = xs_b[:, p_abs * 128:(p_abs + 1) * 128]
                rhs = jnp.concatenate(
                    [jnp.where(lo_half, xp, jnp.zeros_like(xp)),
                     jnp.where(lo_half, jnp.zeros_like(xp), xp)], axis=0)
                ydiag.append(jnp.dot(lhs, rhs, preferred_element_type=F32))
            ydiag = jnp.concatenate(ydiag, axis=1)
            y_parts.append(ydiag + yoff_f * e1[:, sl] + yoff_b * e2[:, sl])
        y = xs * dexp_ref[...] + jnp.concatenate(y_parts, axis=1)

        e3 = replicate(ws_s[hr, :], sel64, 0, H)
        state_update(hf_st, Bm, (xs * e3).astype(BF16), e1[Q - 1:Q, :])

        zg = z_ref[pl.ds(t0, Q), :].astype(F32)
        o_ref[pl.ds(t0, Q), :] = _rms_rows(y * _silu(zg), nw_ref[...]).astype(BF16)
        return carry
    lax.fori_loop(0, nc, fwd_body, 0)


def _ssd(z, xbc, dtc, cw, cb, dtb, alog, dexp, nw, *, nb, L):
    full = lambda a: pl.BlockSpec(a.shape, lambda b: (0,) * a.ndim)
    nc = L // SSD_Q
    return pl.pallas_call(
        functools.partial(_ssd_kernel, L=L),
        grid=(nb,),
        in_specs=[pl.BlockSpec((L, D_MODEL), lambda b: (b, 0)),
                  pl.BlockSpec((L, CONV_CH), lambda b: (b, 0)),
                  pl.BlockSpec((nc * 2 * SSD_HEADS, SSD_Q), lambda b: (b, 0)),
                  full(cw), full(cb), full(dtb), full(alog), full(dexp), full(nw)],
        out_specs=pl.BlockSpec((L, D_MODEL), lambda b: (b, 0)),
        out_shape=jax.ShapeDtypeStruct((nb * L, D_MODEL), BF16),
        scratch_shapes=[pltpu.VMEM((L + 2 * CONV_HALO, CONV_CH), BF16),
                        pltpu.VMEM((L, CONV_CH), BF16),
                        pltpu.VMEM((nc, SSD_GROUPS, 128, 512), BF16),
                        pltpu.VMEM((SSD_GROUPS, 128, 512), F32),
                        pltpu.VMEM((SSD_GROUPS, 128, 512), F32),
                        pltpu.VMEM((3 * 2 * SSD_HEADS, 2 * SSD_HEADS * 128), BF16),
                        pltpu.VMEM((3 * 2 * SSD_HEADS, 2 * SSD_HEADS * 64), BF16)]
                       + [pltpu.VMEM((nc * 2 * SSD_HEADS, SSD_Q), F32)] * 5
                       + [pltpu.VMEM(((D_CONV - 1) * SSD_Q, SSD_Q + 2 * CONV_HALO), BF16)],
        compiler_params=_cparams(("parallel",)),
        name="ssd_mixer",
    )(z, xbc, dtc, cw, cb, dtb, alog, dexp, nw)


def _attn_kernel(q_ref, k_ref, v_ref, g_ref, bt_ref, lq_ref, sw_ref, o_ref, *, L, lambda_init):
    QB = ATT_QB
    nbt = L // BIAS_TILE
    k = k_ref[...]
    v = v_ref[...]
    v_ext = jnp.concatenate([v, jnp.ones_like(v)], axis=1)
    lq = lq_ref[...]
    lam = (jnp.exp(jnp.sum(lq[0:1] * lq[1:2], axis=-1, keepdims=True))
           - jnp.exp(jnp.sum(lq[2:3] * lq[3:4], axis=-1, keepdims=True)) + lambda_init)
    scale = DA_HEAD_DIM ** -0.5
    sizes = [QB] * (L // QB)
    if len(sizes) > 2 and QB // 2 >= BIAS_TILE:
        sizes = [QB // 2] + sizes[:-1] + [QB // 2]

    q0 = 0
    for qb in sizes:
        lo_half = lax.broadcasted_iota(jnp.int32, (qb, 128), 1) < 64
        q = q_ref[q0:q0 + qb, :] * scale
        tiles = [nbt - 1 - (q0 // BIAS_TILE + r) for r in range(qb // BIAS_TILE)]
        bias = jnp.concatenate([bt_ref[0, :, t * BIAS_TILE:t * BIAS_TILE + L] for t in tiles],
                               axis=0)
        outs = []
        for c in range(2):
            qc = jnp.where(lo_half if c == 0 else jnp.logical_not(lo_half), q, jnp.zeros_like(q))
            s = lax.dot_general(qc, k, (((1,), (1,)), ((), ())), preferred_element_type=F32)
            s = s + bias
            m = jnp.max(s, axis=-1, keepdims=True)
            e = jnp.exp(s - m).astype(BF16)
            pv = jnp.dot(e, v_ext, preferred_element_type=F32)
            outs.append(pv[:, 0:128] / pv[:, 128:256])
        o = outs[0] - lam * outs[1]
        o = _rms_rows(o, sw_ref[...]) * (1.0 - lambda_init)
        gg = g_ref[q0:q0 + qb, :].astype(F32)
        o_ref[q0:q0 + qb, :] = (o * _silu(gg)).astype(BF16)
        q0 += qb


def _attn(q, k, v, g, bt, lq, sw, *, nb, L, lambda_init):
    blk = pl.BlockSpec((L, 128), lambda b, h: (b, h))
    return pl.pallas_call(
        functools.partial(_attn_kernel, L=L, lambda_init=lambda_init),
        grid=(nb, DA_HEADS),
        in_specs=[blk, blk, blk, blk,
                  pl.BlockSpec((1,) + bt.shape[1:], lambda b, h: (h, 0, 0)),
                  pl.BlockSpec(lq.shape, lambda b, h: (0, 0)),
                  pl.BlockSpec(sw.shape, lambda b, h: (0, 0))],
        out_specs=blk,
        out_shape=jax.ShapeDtypeStruct((nb * L, D_MODEL), BF16),
        compiler_params=_cparams(("parallel", "parallel")),
        name="diff_attn",
    )(q, k, v, g, bt, lq, sw)


def _t5_bucket(rel):
    half = N_BUCKETS // 2
    max_exact = half // 2
    n = jnp.abs(rel)
    large = max_exact + (jnp.log(jnp.maximum(n, 1).astype(F32) / max_exact)
                         / math.log(MAX_DISTANCE / max_exact) * (half - max_exact)).astype(jnp.int32)
    large = jnp.minimum(large, half - 1)
    return jnp.where(rel > 0, half, 0) + jnp.where(n < max_exact, n, large)


def _bias_tiles(rel_bias, L):
    nbt = L // BIAS_TILE
    r = jnp.arange(BIAS_TILE)[:, None]
    c = jnp.arange((2 * nbt - 1) * BIAS_TILE)[None, :]
    bucket = _t5_bucket(c - r - (nbt - 1) * BIAS_TILE)[None]
    table = rel_bias.astype(F32).T[:, :, None, None]
    out = jnp.zeros((DA_HEADS,) + bucket.shape[1:], F32)
    for b in range(N_BUCKETS):
        out = jnp.where(bucket == b, table[:, b], out)
    return out


def _out_even_kernel(a_ref, b_ref, w_ref, x_ref, o_ref):
    acc = jnp.dot(a_ref[...], w_ref[0:D_MODEL, :], preferred_element_type=F32)
    acc = acc + jnp.dot(b_ref[...], w_ref[D_MODEL:2 * D_MODEL, :], preferred_element_type=F32)
    o_ref[...] = x_ref[...] + acc


def _out_even(a, b, w, x2d):
    T = x2d.shape[0]
    TM = min(TM_MAX, T)
    row = pl.BlockSpec((TM, D_MODEL), lambda i: (i, 0))
    return pl.pallas_call(
        _out_even_kernel,
        grid=(T // TM,),
        in_specs=[row, row, pl.BlockSpec(w.shape, lambda i: (0, 0)), row],
        out_specs=row,
        out_shape=jax.ShapeDtypeStruct((T, D_MODEL), F32),
        compiler_params=_cparams(("parallel",)),
        name="out_even",
    )(a, b, w, x2d)


def _proj_odd_kernel(x_ref, nw_ref, wuT_ref, wz_ref, uT_ref, zc_ref):
    h = _rms_rows(x_ref[...], nw_ref[...]).astype(BF16)
    uT_ref[...] = lax.dot_general(wuT_ref[...], h, (((1,), (1,)), ((), ())),
                                  preferred_element_type=F32).astype(BF16)
    zc_ref[...] = jnp.dot(h, wz_ref[...], preferred_element_type=F32).astype(BF16)


def _proj_odd(x2d, nw, wuT, wz):
    T = x2d.shape[0]
    TM = min(TM_MAX, T)
    row = pl.BlockSpec((TM, D_MODEL), lambda i: (i, 0))
    full = lambda a: pl.BlockSpec(a.shape, lambda i: (0,) * a.ndim)
    return pl.pallas_call(
        _proj_odd_kernel,
        grid=(T // TM,),
        in_specs=[row, full(nw), full(wuT), full(wz)],
        out_specs=[pl.BlockSpec((D_MODEL, TM), lambda i: (0, i)), row],
        out_shape=[jax.ShapeDtypeStruct((D_MODEL, T), BF16),
                   jax.ShapeDtypeStruct((T, D_MODEL), BF16)],
        compiler_params=_cparams(("parallel",)),
        name="proj_odd",
    )(x2d, nw, wuT, wz)


def _s5_kernel(xp_ref, xs_ref, lag_ref, lagn_ref, pm_ref, q_ref, a1_ref, a2_ref, d_ref,
               op_ref, os_ref, m_scr, *, nchunk):
    GS = S5_GROUP
    T = S5_T
    Rp = xp_ref.shape[1]
    R = Rp + xs_ref.shape[1]

    def toeplitz_tile(lag_r, ci, co):
        w = jnp.broadcast_to(lag_r[0, ci, co:co + 1, :], (T, 2 * T))
        return pltpu.roll(w, 1, 1, stride=1, stride_axis=0)[:, T:].astype(BF16)

    g = pl.program_id(0)
    slot = lax.rem(g, 2)

    @pl.when(g == 0)
    def _():
        def build(ci, carry):
            r0 = pl.multiple_of(ci * T, T)
            for co in range(GS):
                m_scr[0, pl.ds(r0, T), co * T:(co + 1) * T] = toeplitz_tile(lag_ref, ci, co)
            return carry
        lax.fori_loop(0, GS, build, 0)

    lhs = jnp.concatenate([jnp.concatenate([xp_ref[c], xs_ref[c]], axis=0) for c in range(GS)],
                          axis=1)
    st = jnp.dot(lhs, pm_ref[0], preferred_element_type=F32)
    a1 = a1_ref[0]
    a2 = a2_ref[0]
    chunk_of_row = lax.broadcasted_iota(jnp.int32, (R, 128), 0) % nchunk

    def chunk_scan(s, cr, ci, reverse):
        def shifted(v, m):
            v = pltpu.roll(v, (R - m) if reverse else m, axis=0)
            ok = (chunk_of_row < nchunk - m) if reverse else (chunk_of_row >= m)
            return jnp.where(ok, v, 0.0)
        h, m = s, 1
        while m < nchunk:
            v = shifted(h, m)
            h = h + v * cr + pltpu.roll(v, 64, axis=1) * ci
            cr, ci = cr * cr - ci * ci, 2.0 * cr * ci
            m *= 2
        return shifted(h, 1)

    hprev = jnp.concatenate(
        [chunk_scan(st[:, d * 128:(d + 1) * 128], a1[:, d * 128:(d + 1) * 128],
                    a2[:, d * 128:(d + 1) * 128], reverse=(d == 1)) for d in range(2)],
        axis=1).astype(BF16)
    y = (jnp.dot(lhs, m_scr[slot], preferred_element_type=F32)
         + jnp.dot(hprev, q_ref[0], preferred_element_type=F32))
    for ci in range(GS):
        for co in range(GS):
            m_scr[1 - slot, ci * T:(ci + 1) * T, co * T:(co + 1) * T] = toeplitz_tile(lagn_ref, ci, co)
    for c in range(GS):
        yc = y[:, c * T:(c + 1) * T]
        dc = d_ref[0, c:c + 1, :]
        op_ref[c] = (yc[:Rp] + dc * xp_ref[c].astype(F32)).astype(BF16)
        os_ref[c] = (yc[Rp:] + dc * xs_ref[c].astype(F32)).astype(BF16)


def _s5(uT3p, uT3s, lag, pm, qm, a1, a2, dexp, *, nchunk):
    Rp, Rs = uT3p.shape[1], uT3s.shape[1]
    xblk = lambda r: pl.BlockSpec((S5_GROUP, r, S5_T), lambda g: (g, 0, 0))
    per_g = lambda a: pl.BlockSpec((1,) + a.shape[1:], lambda g: (g,) + (0,) * (a.ndim - 1))
    KT = S5_GROUP * S5_T
    return pl.pallas_call(
        functools.partial(_s5_kernel, nchunk=nchunk),
        grid=(S5_GROUPS,),
        in_specs=[xblk(Rp), xblk(Rs), per_g(lag),
                  pl.BlockSpec((1,) + lag.shape[1:],
                               lambda g: (jnp.minimum(g + 1, S5_GROUPS - 1), 0, 0, 0)),
                  per_g(pm), per_g(qm), per_g(a1), per_g(a2), per_g(dexp)],
        out_specs=[xblk(Rp), xblk(Rs)],
        out_shape=[jax.ShapeDtypeStruct(uT3p.shape, BF16), jax.ShapeDtypeStruct(uT3s.shape, BF16)],
        scratch_shapes=[pltpu.VMEM((2, KT, KT), BF16)],
        compiler_params=_cparams(("arbitrary",)),
        name="s5_mixer",
    )(uT3p, uT3s, lag, lag, pm, qm, a1, a2, dexp)


def _s5_weights(lam_re, lam_im, log_dt, B_re, B_im, C_re, C_im):
    T = S5_T
    hp = lax.Precision.HIGHEST
    step = jnp.exp(log_dt)[:, :, None]
    lr, li = lam_re, lam_im
    t = jnp.arange(T + 1, dtype=F32)[:, None, None, None]
    pw_mag = jnp.exp(lr * step * t)
    pw_re = pw_mag * jnp.cos(li * step * t)
    pw_im = pw_mag * jnp.sin(li * step * t)
    ab_re, ab_im = pw_re[1], pw_im[1]
    den = lr * lr + li * li
    nr = ab_re - 1.0
    cr = (nr * lr + ab_im * li) / den
    ci = (ab_im * lr - nr * li) / den
    bb_re = cr[..., None] * B_re - ci[..., None] * B_im
    bb_im = cr[..., None] * B_im + ci[..., None] * B_re
    ca_re = C_re[None] * pw_re[:, :, :, None, :] - C_im[None] * pw_im[:, :, :, None, :]
    ca_im = C_re[None] * pw_im[:, :, :, None, :] + C_im[None] * pw_re[:, :, :, None, :]
    kk = (jnp.einsum('tdgcp,dgpk->tdgck', ca_re[:T], bb_re, precision=hp)
          - jnp.einsum('tdgcp,dgpk->tdgck', ca_im[:T], bb_im, precision=hp))
    kf, kb = kk[:, 0], kk[:, 1]
    lagged = jnp.concatenate([kb[:0:-1], (kf[0] + kb[0])[None], kf[1:]], axis=0)
    lag = jnp.pad(jnp.transpose(lagged, (1, 3, 2, 0)), ((0, 0), (0, 0), (0, 0), (0, 1)))
    G = lag.shape[0]
    pf_re, pf_im = pw_re[:T][::-1, 0], pw_im[:T][::-1, 0]
    pb_re, pb_im = pw_re[:T, 1], pw_im[:T, 1]

    def state_proj(p_re, p_im, d):
        s_re = p_re[:, :, :, None] * bb_re[d][None] - p_im[:, :, :, None] * bb_im[d][None]
        s_im = p_re[:, :, :, None] * bb_im[d][None] + p_im[:, :, :, None] * bb_re[d][None]
        s = jnp.concatenate([s_re, s_im], axis=2)
        return jnp.transpose(s, (1, 3, 0, 2))
    pm = jnp.concatenate([state_proj(pf_re, pf_im, 0), state_proj(pb_re, pb_im, 1)], axis=-1)
    pm = pm.reshape(G, S5_GROUP * T, 4 * S5_STATE).astype(BF16)
    def out_proj(re, im):
        o = jnp.concatenate([re, -im], axis=-1)
        return jnp.transpose(o, (1, 3, 2, 0))
    qf = out_proj(ca_re[1:T + 1, 0], ca_im[1:T + 1, 0])
    qb = out_proj(ca_re[T:0:-1, 1], ca_im[T:0:-1, 1])
    qm = jnp.concatenate([qf, qb], axis=1).reshape(G, 4 * S5_STATE, S5_GROUP * T).astype(BF16)
    at_re, at_im = pw_re[T], pw_im[T]
    a1 = jnp.concatenate([at_re[0], at_re[0], at_re[1], at_re[1]], axis=-1)[:, None, :]
    a2 = jnp.concatenate([-at_im[0], at_im[0], -at_im[1], at_im[1]], axis=-1)[:, None, :]
    return lag, pm, qm, a1, a2


def _out_odd_kernel(y_ref, zc_ref, x_ref, wg_ref, bg_ref, wo_ref, fw_ref, o_ref, *, final):
    h = jax.nn.gelu(y_ref[...].astype(F32))
    gate = jax.nn.sigmoid(jnp.dot(h.astype(BF16), wg_ref[...], preferred_element_type=F32)
                          + bg_ref[...])
    t = h * gate * _silu(zc_ref[...].astype(F32))
    xn = x_ref[...] + jnp.dot(t.astype(BF16), wo_ref[...], preferred_element_type=F32)
    o_ref[...] = _rms_rows(xn, fw_ref[...]) if final else xn


def _out_odd(y, zc, x2d, wg, bg, wo, fw, *, final):
    T = x2d.shape[0]
    TM = min(TM_MAX, T)
    row = pl.BlockSpec((TM, D_MODEL), lambda i: (i, 0))
    full = lambda a: pl.BlockSpec(a.shape, lambda i: (0,) * a.ndim)
    return pl.pallas_call(
        functools.partial(_out_odd_kernel, final=final),
        grid=(T // TM,),
        in_specs=[row, row, row, full(wg), full(bg), full(wo), full(fw)],
        out_specs=row,
        out_shape=jax.ShapeDtypeStruct((T, D_MODEL), F32),
        compiler_params=_cparams(("parallel",)),
        name="out_odd",
    )(y, zc, x2d, wg, bg, wo, fw)


def _prep_even(e, w_in_ab, conv_w, conv_b, ssd_dt_bias, ssd_A_log, ssd_D, ssd_norm_w,
               diff_lambda, diff_subln_w, w_out_ab):
    w = w_in_ab[e]
    o_dt = D_MODEL + CONV_CH
    wm = jnp.concatenate([w[:, :o_dt], w[:, o_dt + 2 * SSD_HEADS:]], axis=1).astype(BF16)
    wdt = w[:, o_dt:o_dt + 2 * SSD_HEADS].astype(BF16)
    return dict(
        wm=wm, wdtT=wdt.T,
        cw=conv_w[e], cb=conv_b[e][None, :],
        dtb=ssd_dt_bias[e].reshape(-1, 1), alog=ssd_A_log[e].reshape(-1, 1),
        dexp=jnp.repeat(ssd_D[e], SSD_HEAD_DIM)[None, :],
        snw=ssd_norm_w[e][None, :],
        lq=diff_lambda[e], sw=diff_subln_w[e][None, :],
        wo=w_out_ab[e].astype(BF16))


def _prep_odd(o, w_in_c, s5_lambda_re, s5_lambda_im, s5_log_dt, s5_B_re, s5_B_im, s5_C_re,
              s5_C_im, s5_D, w_glu, b_glu, w_out_c):
    w = w_in_c[o]
    lag, pm, qm, a1, a2 = _s5_weights(s5_lambda_re[o], s5_lambda_im[o], s5_log_dt[o], s5_B_re[o],
                                      s5_B_im[o], s5_C_re[o], s5_C_im[o])
    dexp = jnp.broadcast_to(s5_D[o].reshape(S5_GROUPS, S5_GROUP, 1), (S5_GROUPS, S5_GROUP, S5_T))
    return dict(wuT=w[:, :D_MODEL].T.astype(BF16), wz=w[:, D_MODEL:].astype(BF16),
                lag=lag, pm=pm, qm=qm, a1=a1, a2=a2, dexp=dexp,
                wg=w_glu[o].astype(BF16), bg=b_glu[o][None, :], wo=w_out_c[o].astype(BF16))


def _even_layer(x2d, nw, p, bt, *, nb, L, lambda_init):
    z, xbc, q, k, v, g, dtc = _proj_even(x2d, nw, p["wm"], p["wdtT"])
    tile = lambda col: jnp.tile(col, (L // SSD_Q, SSD_Q))
    y_ssd = _ssd(z, xbc, dtc, p["cw"], p["cb"], tile(p["dtb"]), tile(p["alog"]), p["dexp"],
                 p["snw"], nb=nb, L=L)
    y_att = _attn(q, k, v, g, bt, p["lq"], p["sw"], nb=nb, L=L, lambda_init=lambda_init)
    return _out_even(y_ssd, y_att, p["wo"], x2d)


def _odd_layer(xs, nw, p, fw, *, L, final):
    proj = [_proj_odd(x2d, nw, p["wuT"], p["wz"]) for x2d in xs]
    uT3 = [uT.reshape(D_MODEL, uT.shape[1] // S5_T, S5_T) for uT, _ in proj]
    yT3 = _s5(uT3[0], uT3[1], p["lag"], p["pm"], p["qm"], p["a1"], p["a2"], p["dexp"],
              nchunk=L // S5_T)
    out = []
    for x2d, (_, zc), y3 in zip(xs, proj, yT3):
        y = jnp.transpose(y3, (1, 2, 0)).reshape(x2d.shape)
        out.append(_out_odd(y, zc, x2d, p["wg"], p["bg"], p["wo"], fw, final=final))
    return out


def kernel(x_prompt, x_sample, norm_w, final_norm_w, rel_bias, w_in_ab, conv_w, conv_b,
           ssd_dt_bias, ssd_A_log, ssd_D, ssd_norm_w, diff_lambda, diff_subln_w, w_out_ab,
           w_in_c, s5_lambda_re, s5_lambda_im, s5_log_dt, s5_B_re, s5_B_im, s5_C_re, s5_C_im,
           s5_D, w_glu, b_glu, w_out_c):
    even = [_prep_even(e, w_in_ab, conv_w, conv_b, ssd_dt_bias, ssd_A_log, ssd_D, ssd_norm_w,
                       diff_lambda, diff_subln_w, w_out_ab) for e in range((DEPTH + 1) // 2)]
    odd = [_prep_odd(o, w_in_c, s5_lambda_re, s5_lambda_im, s5_log_dt, s5_B_re, s5_B_im,
                     s5_C_re, s5_C_im, s5_D, w_glu, b_glu, w_out_c) for o in range(DEPTH // 2)]
    fw = final_norm_w[None, :]
    L = x_prompt.shape[1]
    assert x_sample.shape[1] == L
    bt = _bias_tiles(rel_bias, L)
    shapes = [x_prompt.shape, x_sample.shape]
    xs = [x.reshape(-1, D_MODEL) for x in (x_prompt, x_sample)]
    for l in range(DEPTH):
        nw = norm_w[l][None, :]
        if l % 2 == 0:
            lambda_init = 0.8 - 0.6 * math.exp(-0.3 * l)
            xs = [_even_layer(x2d, nw, even[l // 2], bt, nb=shp[0], L=L, lambda_init=lambda_init)
                  for x2d, shp in zip(xs, shapes)]
        else:
            xs = _odd_layer(xs, nw, odd[l // 2], fw, L=L, final=(l == DEPTH - 1))
    return tuple(x2d.reshape(shp) for x2d, shp in zip(xs, shapes))
```

```python
import functools
import math

import jax
import jax.numpy as jnp
from jax import lax
from jax.experimental import pallas as pl
from jax.experimental.pallas import tpu as pltpu

F32 = jnp.float32
BF16 = jnp.bfloat16

D_MODEL = 1024
DEPTH = 4
EPS = 1e-5

SSD_HEADS = 16
SSD_HEAD_DIM = 64
SSD_GROUPS = 2
SSD_STATE = 64
D_CONV = 5
CONV_CH = D_MODEL + 2 * SSD_GROUPS * SSD_STATE
SSD_Q = 128
CONV_HALO = 16

DA_HEADS = 8
DA_HEAD_DIM = 64
N_BUCKETS = 32
MAX_DISTANCE = 128
ATT_QB = 128
BIAS_TILE = 128

S5_GROUP = 16
S5_GROUPS = 64
S5_STATE = 64
S5_T = 128

TM_MAX = 512
VMEM_LIMIT = 56 * 1024 * 1024


def _cparams(sem):
    return pltpu.CompilerParams(dimension_semantics=sem, vmem_limit_bytes=VMEM_LIMIT)


def _silu(x):
    return x * jax.nn.sigmoid(x)


def _rms_rows(x, w):
    ms = jnp.mean(x * x, axis=-1, keepdims=True)
    return x * lax.rsqrt(ms + EPS) * w


EVEN_SPLITS = (("z", 0, 1024), ("xbc", 1024, 1280), ("q", 2304, 1024), ("k", 3328, 1024),
               ("v", 4352, 1024), ("g", 5376, 1024))
EVEN_MAIN = 6400


def _proj_even_kernel(x_ref, nw_ref, wm_ref, wdtT_ref,
                      z_ref, xbc_ref, q_ref, k_ref, v_ref, g_ref, dtc_ref):
    h = _rms_rows(x_ref[...], nw_ref[...]).astype(BF16)
    outs = (z_ref, xbc_ref, q_ref, k_ref, v_ref, g_ref)
    for ref, (_, off, n) in zip(outs, EVEN_SPLITS):
        ref[...] = jnp.dot(h, wm_ref[:, off:off + n], preferred_element_type=F32).astype(BF16)
    dtT = lax.dot_general(wdtT_ref[...], h, (((1,), (1,)), ((), ())),
                          preferred_element_type=F32)
    nh = dtT.shape[0]
    for j in range(dtT.shape[1] // SSD_Q):
        dtc_ref[j * nh:(j + 1) * nh, :] = dtT[:, j * SSD_Q:(j + 1) * SSD_Q]


def _proj_even(x2d, nw, wm, wdtT):
    T = x2d.shape[0]
    TM = min(TM_MAX, T)
    row = lambda n: pl.BlockSpec((TM, n), lambda i: (i, 0))
    full = lambda a: pl.BlockSpec(a.shape, lambda i: (0,) * a.ndim)
    out_shape = [jax.ShapeDtypeStruct((T, n), BF16) for (_, _, n) in EVEN_SPLITS]
    nh = wdtT.shape[0]
    out_shape += [jax.ShapeDtypeStruct((T // SSD_Q * nh, SSD_Q), F32)]
    out_specs = [row(n) for (_, _, n) in EVEN_SPLITS]
    out_specs += [pl.BlockSpec((TM // SSD_Q * nh, SSD_Q), lambda i: (i, 0))]
    return pl.pallas_call(
        _proj_even_kernel,
        grid=(T // TM,),
        in_specs=[row(D_MODEL), full(nw), full(wm), full(wdtT)],
        out_specs=out_specs,
        out_shape=out_shape,
        compiler_params=_cparams(("parallel",)),
        name="proj_even",
    )(x2d, nw, wm, wdtT)


def _ssd_kernel(z_ref, xbc_ref, dtc_ref, cw_ref, cb_ref, dtb_ref, alog_ref, dexp_ref, nw_ref,
                o_ref, xpad, xconv, hb_all, hf_st, hb_st, sel128, sel64,
                dt_s, cs_s, rt_s, ws_s, ec_s, shift_m, *, L):
    Q = SSD_Q
    H = SSD_HEADS
    nc = L // Q
    NP = H // 2

    zeros_h = jnp.zeros((CONV_HALO, CONV_CH), BF16)
    xpad[0:CONV_HALO, :] = zeros_h
    xpad[L + CONV_HALO:L + 2 * CONV_HALO, :] = zeros_h

    def stage_body(c, carry):
        t0 = pl.multiple_of(c * Q, Q)
        xpad[pl.ds(t0 + CONV_HALO, Q), :] = xbc_ref[pl.ds(t0, Q), :]
        return carry
    lax.fori_loop(0, nc, stage_body, 0)

    centre = D_CONV // 2
    taps = [kk for kk in range(D_CONV) if kk != centre]
    sr = lax.broadcasted_iota(jnp.int32, (Q, Q + 2 * CONV_HALO), 0)
    sc = lax.broadcasted_iota(jnp.int32, (Q, Q + 2 * CONV_HALO), 1)
    for idx, kk in enumerate(taps):
        shift_m[idx * Q:(idx + 1) * Q, :] = jnp.where(
            sc == sr + (CONV_HALO + kk - centre), 1.0, 0.0).astype(BF16)

    def conv_body(c, carry):
        t0 = pl.multiple_of(c * Q, Q)
        slab = xpad[pl.ds(t0, Q + 2 * CONV_HALO), :]
        shifted = jnp.dot(shift_m[...], slab, preferred_element_type=F32)
        acc = cb_ref[...] + slab[CONV_HALO:CONV_HALO + Q, :].astype(F32) * cw_ref[centre:centre + 1, :]
        for idx, kk in enumerate(taps):
            acc = acc + shifted[idx * Q:(idx + 1) * Q, :] * cw_ref[kk:kk + 1, :]
        xconv[pl.ds(t0, Q), :] = _silu(acc).astype(BF16)
        return carry
    lax.fori_loop(0, nc, conv_body, 0)

    ri = lax.broadcasted_iota(jnp.int32, (Q, Q), 0)
    ci = lax.broadcasted_iota(jnp.int32, (Q, Q), 1)
    lower_incl = (ri >= ci).astype(F32)
    upper_incl = (ci >= ri).astype(F32)
    strict_lower = ri > ci
    eye = ri == ci
    lane128 = lax.broadcasted_iota(jnp.int32, (Q, 128), 1)
    lo_half = lane128 < 64
    hi = lax.Precision.HIGHEST

    is_fwd = lax.broadcasted_iota(jnp.int32, (nc * 2 * H, Q), 0) % (2 * H) < H
    dt_all = jax.nn.softplus(dtc_ref[...] + dtb_ref[...])
    a_all = dt_all * -jnp.exp(alog_ref[...])
    pre = jnp.dot(a_all, upper_incl, precision=hi, preferred_element_type=F32)
    suf = jnp.dot(a_all, lower_incl, precision=hi, preferred_element_type=F32)
    cs_all = jnp.where(is_fwd, pre, suf)
    tot = jnp.where(is_fwd[:, 0:1], cs_all[:, Q - 1:Q], cs_all[:, 0:1])
    dt_s[...] = dt_all
    cs_s[...] = cs_all
    rt_s[...] = jnp.log(dt_all) - cs_all
    ws_s[...] = jnp.exp(tot - cs_all) * dt_all
    ec_s[...] = jnp.exp(cs_all)

    for sel_ref, width in ((sel128, 128), (sel64, 64)):
        n = 2 * H * width
        src = lax.broadcasted_iota(jnp.int32, (3 * 2 * H, n), 0) % (2 * H)
        dst = lax.broadcasted_iota(jnp.int32, (3 * 2 * H, n), 1) // width
        sel_ref[...] = jnp.where(src == dst, 1.0, 0.0).astype(BF16)

    def replicate(rows, sel_ref, lo, hi_row):
        width = sel_ref.shape[1] // (2 * H)
        p1 = rows.astype(BF16)
        r1 = rows - p1.astype(F32)
        p2 = r1.astype(BF16)
        p3 = (r1 - p2.astype(F32)).astype(BF16)
        pieces = jnp.concatenate([p1, p2, p3], axis=0)
        return lax.dot_general(pieces, sel_ref[:, lo * width:hi_row * width],
                               (((0,), (0,)), ((), ())), preferred_element_type=F32)

    def group_mask(x128, g):
        keep = lo_half if g == 0 else jnp.logical_not(lo_half)
        return jnp.where(keep, x128, jnp.zeros_like(x128))

    def state_update(st_ref, Bm, xs_scaled, decay_row):
        for g in range(SSD_GROUPS):
            Bg = group_mask(Bm, g)
            dS = lax.dot_general(Bg, xs_scaled[:, g * 512:(g + 1) * 512],
                                 (((0,), (0,)), ((), ())), preferred_element_type=F32)
            st_ref[g] = st_ref[g] * decay_row[:, g * 512:(g + 1) * 512] + dS

    hb_st[...] = jnp.zeros_like(hb_st)

    def bwd_body(cc, carry):
        c = nc - 1 - cc
        t0 = pl.multiple_of(c * Q, Q)
        hb_all[c] = hb_st[...].astype(BF16)
        hr = pl.ds(pl.multiple_of(c * 2 * H, 2 * H), 2 * H)
        e4 = replicate(ws_s[hr, :], sel64, H, 2 * H)
        decay = replicate(ec_s[hr, :], sel64, H, 2 * H)[0:1, :]
        xs = xconv[pl.ds(t0, Q), 0:D_MODEL].astype(F32)
        Bm = xconv[pl.ds(t0, Q), D_MODEL:D_MODEL + 128]
        state_update(hb_st, Bm, (xs * e4).astype(BF16), decay)
        return carry
    lax.fori_loop(0, nc, bwd_body, 0)

    hf_st[...] = jnp.zeros_like(hf_st)

    def fwd_body(c, carry):
        t0 = pl.multiple_of(c * Q, Q)
        hr = pl.ds(pl.multiple_of(c * 2 * H, 2 * H), 2 * H)
        dt_r = dt_s[hr, :]
        rterm = rt_s[hr, :]
        xs_b = xconv[pl.ds(t0, Q), 0:D_MODEL]
        xs = xs_b.astype(F32)
        Bm = xconv[pl.ds(t0, Q), D_MODEL:D_MODEL + 128]
        Cm = xconv[pl.ds(t0, Q), D_MODEL + 128:D_MODEL + 256]
        e12 = replicate(ec_s[hr, :], sel64, 0, 2 * H)
        e1 = e12[:, :D_MODEL]
        e2 = e12[:, D_MODEL:]
        colrep = replicate(cs_s[hr, :], sel128, 0, 2 * H)

        y_parts = []
        for g in range(SSD_GROUPS):
            Cg = group_mask(Cm, g)
            cbm = lax.dot_general(Cg, Bm, (((1,), (1,)), ((), ())),
                                  preferred_element_type=F32)
            yoff_f = jnp.dot(Cg, hf_st[g].astype(BF16), preferred_element_type=F32)
            yoff_b = jnp.dot(Cg, hb_all[c, g], preferred_element_type=F32)
            sl = slice(g * 512, (g + 1) * 512)
            ydiag = []
            for pp in range(NP // SSD_GROUPS):
                mats = []
                for hh in range(2):
                    h = g * (H // SSD_GROUPS) + 2 * pp + hh
                    ef = colrep[:, h * Q:(h + 1) * Q] + rterm[h:h + 1, :]
                    eb = colrep[:, (H + h) * Q:(H + h + 1) * Q] + rterm[H + h:H + h + 1, :]
                    w = jnp.exp(jnp.where(strict_lower, ef, eb))
                    w = w + jnp.where(eye, dt_r[h:h + 1, :], 0.0)
                    mats.append((cbm * w).astype(BF16))
                lhs = jnp.concatenate(mats, axis=1)
                p_abs = g * (NP // SSD_GROUPS) + pp
                xp = xs_b[:, p_abs * 128:(p_abs + 1) * 128]
                rhs = jnp.concatenate(
                    [jnp.where(lo_half, xp, jnp.zeros_like(xp)),
                     jnp.where(lo_half, jnp.zeros_like(xp), xp)], axis=0)
                ydiag.append(jnp.dot(lhs, rhs, preferred_element_type=F32))
            ydiag = jnp.concatenate(ydiag, axis=1)
            y_parts.append(ydiag + yoff_f * e1[:, sl] + yoff_b * e2[:, sl])
        y = xs * dexp_ref[...] + jnp.concatenate(y_parts, axis=1)

        e3 = replicate(ws_s[hr, :], sel64, 0, H)
        state_update(hf_st, Bm, (xs * e3).astype(BF16), e1[Q - 1:Q, :])

        zg = z_ref[pl.ds(t0, Q), :].astype(F32)
        o_ref[pl.ds(t0, Q), :] = _rms_rows(y * _silu(zg), nw_ref[...]).astype(BF16)
        return carry
    lax.fori_loop(0, nc, fwd_body, 0)


def _ssd(z, xbc, dtc, cw, cb, dtb, alog, dexp, nw, *, nb, L):
    full = lambda a: pl.BlockSpec(a.shape, lambda b: (0,) * a.ndim)
    nc = L // SSD_Q
    return pl.pallas_call(
        functools.partial(_ssd_kernel, L=L),
        grid=(nb,),
        in_specs=[pl.BlockSpec((L, D_MODEL), lambda b: (b, 0)),
                  pl.BlockSpec((L, CONV_CH), lambda b: (b, 0)),
                  pl.BlockSpec((nc * 2 * SSD_HEADS, SSD_Q), lambda b: (b, 0)),
                  full(cw), full(cb), full(dtb), full(alog), full(dexp), full(nw)],
        out_specs=pl.BlockSpec((L, D_MODEL), lambda b: (b, 0)),
        out_shape=jax.ShapeDtypeStruct((nb * L, D_MODEL), BF16),
        scratch_shapes=[pltpu.VMEM((L + 2 * CONV_HALO, CONV_CH), BF16),
                        pltpu.VMEM((L, CONV_CH), BF16),
                        pltpu.VMEM((nc, SSD_GROUPS, 128, 512), BF16),
                        pltpu.VMEM((SSD_GROUPS, 128, 512), F32),
                        pltpu.VMEM((SSD_GROUPS, 128, 512), F32),
                        pltpu.VMEM((3 * 2 * SSD_HEADS, 2 * SSD_HEADS * 128), BF16),
                        pltpu.VMEM((3 * 2 * SSD_HEADS, 2 * SSD_HEADS * 64), BF16)]
                       + [pltpu.VMEM((nc * 2 * SSD_HEADS, SSD_Q), F32)] * 5
                       + [pltpu.VMEM(((D_CONV - 1) * SSD_Q, SSD_Q + 2 * CONV_HALO), BF16)],
        compiler_params=_cparams(("parallel",)),
        name="ssd_mixer",
    )(z, xbc, dtc, cw, cb, dtb, alog, dexp, nw)


def _attn_kernel(q_ref, k_ref, v_ref, g_ref, bt_ref, lq_ref, sw_ref, o_ref, *, L, lambda_init):
    QB = ATT_QB
    nbt = L // BIAS_TILE
    k = k_ref[...]
    v = v_ref[...]
    v_ext = jnp.concatenate([v, jnp.ones_like(v)], axis=1)
    lq = lq_ref[...]
    lam = (jnp.exp(jnp.sum(lq[0:1] * lq[1:2], axis=-1, keepdims=True))
           - jnp.exp(jnp.sum(lq[2:3] * lq[3:4], axis=-1, keepdims=True)) + lambda_init)
    scale = DA_HEAD_DIM ** -0.5
    sizes = [QB] * (L // QB)
    if len(sizes) > 2 and QB // 2 >= BIAS_TILE:
        sizes = [QB // 2] + sizes[:-1] + [QB // 2]

    q0 = 0
    for qb in sizes:
        lo_half = lax.broadcasted_iota(jnp.int32, (qb, 128), 1) < 64
        q = q_ref[q0:q0 + qb, :] * scale
        tiles = [nbt - 1 - (q0 // BIAS_TILE + r) for r in range(qb // BIAS_TILE)]
        bias = jnp.concatenate([bt_ref[0, :, t * BIAS_TILE:t * BIAS_TILE + L] for t in tiles],
                               axis=0)
        outs = []
        for c in range(2):
            qc = jnp.where(lo_half if c == 0 else jnp.logical_not(lo_half), q, jnp.zeros_like(q))
            s = lax.dot_general(qc, k, (((1,), (1,)), ((), ())), preferred_element_type=F32)
            s = s + bias
            m = jnp.max(s, axis=-1, keepdims=True)
            e = jnp.exp(s - m).astype(BF16)
            pv = jnp.dot(e, v_ext, preferred_element_type=F32)
            outs.append(pv[:, 0:128] / pv[:, 128:256])
        o = outs[0] - lam * outs[1]
        o = _rms_rows(o, sw_ref[...]) * (1.0 - lambda_init)
        gg = g_ref[q0:q0 + qb, :].astype(F32)
        o_ref[q0:q0 + qb, :] = (o * _silu(gg)).astype(BF16)
        q0 += qb


def _attn(q, k, v, g, bt, lq, sw, *, nb, L, lambda_init):
    blk = pl.BlockSpec((L, 128), lambda b, h: (b, h))
    return pl.pallas_call(
        functools.partial(_attn_kernel, L=L, lambda_init=lambda_init),
        grid=(nb, DA_HEADS),
        in_specs=[blk, blk, blk, blk,
                  pl.BlockSpec((1,) + bt.shape[1:], lambda b, h: (h, 0, 0)),
                  pl.BlockSpec(lq.shape, lambda b, h: (0, 0)),
                  pl.BlockSpec(sw.shape, lambda b, h: (0, 0))],
        out_specs=blk,
        out_shape=jax.ShapeDtypeStruct((nb * L, D_MODEL), BF16),
        compiler_params=_cparams(("parallel", "parallel")),
        name="diff_attn",
    )(q, k, v, g, bt, lq, sw)


def _t5_bucket(rel):
    half = N_BUCKETS // 2
    max_exact = half // 2
    n = jnp.abs(rel)
    large = max_exact + (jnp.log(jnp.maximum(n, 1).astype(F32) / max_exact)
                         / math.log(MAX_DISTANCE / max_exact) * (half - max_exact)).astype(jnp.int32)
    large = jnp.minimum(large, half - 1)
    return jnp.where(rel > 0, half, 0) + jnp.where(n < max_exact, n, large)


def _bias_tiles(rel_bias, L):
    nbt = L // BIAS_TILE
    r = jnp.arange(BIAS_TILE)[:, None]
    c = jnp.arange((2 * nbt - 1) * BIAS_TILE)[None, :]
    bucket = _t5_bucket(c - r - (nbt - 1) * BIAS_TILE)[None]
    table = rel_bias.astype(F32).T[:, :, None, None]
    out = jnp.zeros((DA_HEADS,) + bucket.shape[1:], F32)
    for b in range(N_BUCKETS):
        out = jnp.where(bucket == b, table[:, b], out)
    return out


def _out_even_kernel(a_ref, b_ref, w_ref, x_ref, o_ref):
    acc = jnp.dot(a_ref[...], w_ref[0:D_MODEL, :], preferred_element_type=F32)
    acc = acc + jnp.dot(b_ref[...], w_ref[D_MODEL:2 * D_MODEL, :], preferred_element_type=F32)
    o_ref[...] = x_ref[...] + acc


def _out_even(a, b, w, x2d):
    T = x2d.shape[0]
    TM = min(TM_MAX, T)
    row = pl.BlockSpec((TM, D_MODEL), lambda i: (i, 0))
    return pl.pallas_call(
        _out_even_kernel,
        grid=(T // TM,),
        in_specs=[row, row, pl.BlockSpec(w.shape, lambda i: (0, 0)), row],
        out_specs=row,
        out_shape=jax.ShapeDtypeStruct((T, D_MODEL), F32),
        compiler_params=_cparams(("parallel",)),
        name="out_even",
    )(a, b, w, x2d)


def _proj_odd_kernel(x_ref, nw_ref, wuT_ref, wz_ref, uT_ref, zc_ref):
    h = _rms_rows(x_ref[...], nw_ref[...]).astype(BF16)
    uT_ref[...] = lax.dot_general(wuT_ref[...], h, (((1,), (1,)), ((), ())),
                                  preferred_element_type=F32).astype(BF16)
    zc_ref[...] = jnp.dot(h, wz_ref[...], preferred_element_type=F32).astype(BF16)


def _proj_odd(x2d, nw, wuT, wz):
    T = x2d.shape[0]
    TM = min(TM_MAX, T)
    row = pl.BlockSpec((TM, D_MODEL), lambda i: (i, 0))
    full = lambda a: pl.BlockSpec(a.shape, lambda i: (0,) * a.ndim)
    return pl.pallas_call(
        _proj_odd_kernel,
        grid=(T // TM,),
        in_specs=[row, full(nw), full(wuT), full(wz)],
        out_specs=[pl.BlockSpec((D_MODEL, TM), lambda i: (0, i)), row],
        out_shape=[jax.ShapeDtypeStruct((D_MODEL, T), BF16),
                   jax.ShapeDtypeStruct((T, D_MODEL), BF16)],
        compiler_params=_cparams(("parallel",)),
        name="proj_odd",
    )(x2d, nw, wuT, wz)


def _s5_kernel(xp_ref, xs_ref, lag_ref, lagn_ref, pm_ref, q_ref, a1_ref, a2_ref, d_ref,
               op_ref, os_ref, m_scr, *, nchunk):
    GS = S5_GROUP
    T = S5_T
    Rp = xp_ref.shape[1]
    R = Rp + xs_ref.shape[1]

    def toeplitz_tile(lag_r, ci, co):
        w = jnp.broadcast_to(lag_r[0, ci, co:co + 1, :], (T, 2 * T))
        return pltpu.roll(w, 1, 1, stride=1, stride_axis=0)[:, T:].astype(BF16)

    g = pl.program_id(0)
    slot = lax.rem(g, 2)

    @pl.when(g == 0)
    def _():
        def build(ci, carry):
            r0 = pl.multiple_of(ci * T, T)
            for co in range(GS):
                m_scr[0, pl.ds(r0, T), co * T:(co + 1) * T] = toeplitz_tile(lag_ref, ci, co)
            return carry
        lax.fori_loop(0, GS, build, 0)

    lhs = jnp.concatenate([jnp.concatenate([xp_ref[c], xs_ref[c]], axis=0) for c in range(GS)],
                          axis=1)
    st = jnp.dot(lhs, pm_ref[0], preferred_element_type=F32)
    a1 = a1_ref[0]
    a2 = a2_ref[0]
    chunk_of_row = lax.broadcasted_iota(jnp.int32, (R, 128), 0) % nchunk

    def chunk_scan(s, cr, ci, reverse):
        def shifted(v, m):
            v = pltpu.roll(v, (R - m) if reverse else m, axis=0)
            ok = (chunk_of_row < nchunk - m) if reverse else (chunk_of_row >= m)
            return jnp.where(ok, v, 0.0)
        h, m = s, 1
        while m < nchunk:
            v = shifted(h, m)
            h = h + v * cr + pltpu.roll(v, 64, axis=1) * ci
            cr, ci = cr * cr - ci * ci, 2.0 * cr * ci
            m *= 2
        return shifted(h, 1)

    hprev = jnp.concatenate(
        [chunk_scan(st[:, d * 128:(d + 1) * 128], a1[:, d * 128:(d + 1) * 128],
                    a2[:, d * 128:(d + 1) * 128], reverse=(d == 1)) for d in range(2)],
        axis=1).astype(BF16)
    y = (jnp.dot(lhs, m_scr[slot], preferred_element_type=F32)
         + jnp.dot(hprev, q_ref[0], preferred_element_type=F32))
    for ci in range(GS):
        for co in range(GS):
            m_scr[1 - slot, ci * T:(ci + 1) * T, co * T:(co + 1) * T] = toeplitz_tile(lagn_ref, ci, co)
    for c in range(GS):
        yc = y[:, c * T:(c + 1) * T]
        dc = d_ref[0, c:c + 1, :]
        op_ref[c] = (yc[:Rp] + dc * xp_ref[c].astype(F32)).astype(BF16)
        os_ref[c] = (yc[Rp:] + dc * xs_ref[c].astype(F32)).astype(BF16)


def _s5(uT3p, uT3s, lag, pm, qm, a1, a2, dexp, *, nchunk):
    Rp, Rs = uT3p.shape[1], uT3s.shape[1]
    xblk = lambda r: pl.BlockSpec((S5_GROUP, r, S5_T), lambda g: (g, 0, 0))
    per_g = lambda a: pl.BlockSpec((1,) + a.shape[1:], lambda g: (g,) + (0,) * (a.ndim - 1))
    KT = S5_GROUP * S5_T
    return pl.pallas_call(
        functools.partial(_s5_kernel, nchunk=nchunk),
        grid=(S5_GROUPS,),
        in_specs=[xblk(Rp), xblk(Rs), per_g(lag),
                  pl.BlockSpec((1,) + lag.shape[1:],
                               lambda g: (jnp.minimum(g + 1, S5_GROUPS - 1), 0, 0, 0)),
                  per_g(pm), per_g(qm), per_g(a1), per_g(a2), per_g(dexp)],
        out_specs=[xblk(Rp), xblk(Rs)],
        out_shape=[jax.ShapeDtypeStruct(uT3p.shape, BF16), jax.ShapeDtypeStruct(uT3s.shape, BF16)],
        scratch_shapes=[pltpu.VMEM((2, KT, KT), BF16)],
        compiler_params=_cparams(("arbitrary",)),
        name="s5_mixer",
    )(uT3p, uT3s, lag, lag, pm, qm, a1, a2, dexp)


def _s5_weights(lam_re, lam_im, log_dt, B_re, B_im, C_re, C_im):
    T = S5_T
    hp = lax.Precision.HIGHEST
    step = jnp.exp(log_dt)[:, :, None]
    lr, li = lam_re, lam_im
    t = jnp.arange(T + 1, dtype=F32)[:, None, None, None]
    pw_mag = jnp.exp(lr * step * t)
    pw_re = pw_mag * jnp.cos(li * step * t)
    pw_im = pw_mag * jnp.sin(li * step * t)
    ab_re, ab_im = pw_re[1], pw_im[1]
    den = lr * lr + li * li
    nr = ab_re - 1.0
    cr = (nr * lr + ab_im * li) / den
    ci = (ab_im * lr - nr * li) / den
    bb_re = cr[..., None] * B_re - ci[..., None] * B_im
    bb_im = cr[..., None] * B_im + ci[..., None] * B_re
    ca_re = C_re[None] * pw_re[:, :, :, None, :] - C_im[None] * pw_im[:, :, :, None, :]
    ca_im = C_re[None] * pw_im[:, :, :, None, :] + C_im[None] * pw_re[:, :, :, None, :]
    kk = (jnp.einsum('tdgcp,dgpk->tdgck', ca_re[:T], bb_re, precision=hp)
          - jnp.einsum('tdgcp,dgpk->tdgck', ca_im[:T], bb_im, precision=hp))
    kf, kb = kk[:, 0], kk[:, 1]
    lagged = jnp.concatenate([kb[:0:-1], (kf[0] + kb[0])[None], kf[1:]], axis=0)
    lag = jnp.pad(jnp.transpose(lagged, (1, 3, 2, 0)), ((0, 0), (0, 0), (0, 0), (0, 1)))
    G = lag.shape[0]
    pf_re, pf_im = pw_re[:T][::-1, 0], pw_im[:T][::-1, 0]
    pb_re, pb_im = pw_re[:T, 1], pw_im[:T, 1]

    def state_proj(p_re, p_im, d):
        s_re = p_re[:, :, :, None] * bb_re[d][None] - p_im[:, :, :, None] * bb_im[d][None]
        s_im = p_re[:, :, :, None] * bb_im[d][None] + p_im[:, :, :, None] * bb_re[d][None]
        s = jnp.concatenate([s_re, s_im], axis=2)
        return jnp.transpose(s, (1, 3, 0, 2))
    pm = jnp.concatenate([state_proj(pf_re, pf_im, 0), state_proj(pb_re, pb_im, 1)], axis=-1)
    pm = pm.reshape(G, S5_GROUP * T, 4 * S5_STATE).astype(BF16)
    def out_proj(re, im):
        o = jnp.concatenate([re, -im], axis=-1)
        return jnp.transpose(o, (1, 3, 2, 0))
    qf = out_proj(ca_re[1:T + 1, 0], ca_im[1:T + 1, 0])
    qb = out_proj(ca_re[T:0:-1, 1], ca_im[T:0:-1, 1])
    qm = jnp.concatenate([qf, qb], axis=1).reshape(G, 4 * S5_STATE, S5_GROUP * T).astype(BF16)
    at_re, at_im = pw_re[T], pw_im[T]
    a1 = jnp.concatenate([at_re[0], at_re[0], at_re[1], at_re[1]], axis=-1)[:, None, :]
    a2 = jnp.concatenate([-at_im[0], at_im[0], -at_im[1], at_im[1]], axis=-1)[:, None, :]
    return lag, pm, qm, a1, a2


def _out_odd_kernel(y_ref, zc_ref, x_ref, wg_ref, bg_ref, wo_ref, fw_ref, o_ref, *, final):
    h = jax.nn.gelu(y_ref[...].astype(F32))
    gate = jax.nn.sigmoid(jnp.dot(h.astype(BF16), wg_ref[...], preferred_element_type=F32)
                          + bg_ref[...])
    t = h * gate * _silu(zc_ref[...].astype(F32))
    xn = x_ref[...] + jnp.dot(t.astype(BF16), wo_ref[...], preferred_element_type=F32)
    o_ref[...] = _rms_rows(xn, fw_ref[...]) if final else xn


def _out_odd(y, zc, x2d, wg, bg, wo, fw, *, final):
    T = x2d.shape[0]
    TM = min(TM_MAX, T)
    row = pl.BlockSpec((TM, D_MODEL), lambda i: (i, 0))
    full = lambda a: pl.BlockSpec(a.shape, lambda i: (0,) * a.ndim)
    return pl.pallas_call(
        functools.partial(_out_odd_kernel, final=final),
        grid=(T // TM,),
        in_specs=[row, row, row, full(wg), full(bg), full(wo), full(fw)],
        out_specs=row,
        out_shape=jax.ShapeDtypeStruct((T, D_MODEL), F32),
        compiler_params=_cparams(("parallel",)),
        name="out_odd",
    )(y, zc, x2d, wg, bg, wo, fw)


def _prep_even(e, w_in_ab, conv_w, conv_b, ssd_dt_bias, ssd_A_log, ssd_D, ssd_norm_w,
               diff_lambda, diff_subln_w, w_out_ab):
    w = w_in_ab[e]
    o_dt = D_MODEL + CONV_CH
    wm = jnp.concatenate([w[:, :o_dt], w[:, o_dt + 2 * SSD_HEADS:]], axis=1).astype(BF16)
    wdt = w[:, o_dt:o_dt + 2 * SSD_HEADS].astype(BF16)
    return dict(
        wm=wm, wdtT=wdt.T,
        cw=conv_w[e], cb=conv_b[e][None, :],
        dtb=ssd_dt_bias[e].reshape(-1, 1), alog=ssd_A_log[e].reshape(-1, 1),
        dexp=jnp.repeat(ssd_D[e], SSD_HEAD_DIM)[None, :],
        snw=ssd_norm_w[e][None, :],
        lq=diff_lambda[e], sw=diff_subln_w[e][None, :],
        wo=w_out_ab[e].astype(BF16))


def _prep_odd(o, s5w, w_in_c, s5_D, w_glu, b_glu, w_out_c):
    w = w_in_c[o]
    lag, pm, qm, a1, a2 = (t[o] for t in s5w)
    dexp = jnp.broadcast_to(s5_D[o].reshape(S5_GROUPS, S5_GROUP, 1), (S5_GROUPS, S5_GROUP, S5_T))
    return dict(wuT=w[:, :D_MODEL].T.astype(BF16), wz=w[:, D_MODEL:].astype(BF16),
                lag=lag, pm=pm, qm=qm, a1=a1, a2=a2, dexp=dexp,
                wg=w_glu[o].astype(BF16), bg=b_glu[o][None, :], wo=w_out_c[o].astype(BF16))


def _even_layer(x2d, nw, p, bt, *, nb, L, lambda_init):
    z, xbc, q, k, v, g, dtc = _proj_even(x2d, nw, p["wm"], p["wdtT"])
    tile = lambda col: jnp.tile(col, (L // SSD_Q, SSD_Q))
    y_ssd = _ssd(z, xbc, dtc, p["cw"], p["cb"], tile(p["dtb"]), tile(p["alog"]), p["dexp"],
                 p["snw"], nb=nb, L=L)
    y_att = _attn(q, k, v, g, bt, p["lq"], p["sw"], nb=nb, L=L, lambda_init=lambda_init)
    return _out_even(y_ssd, y_att, p["wo"], x2d)


def _odd_layer(xs, nw, p, fw, *, L, final):
    proj = [_proj_odd(x2d, nw, p["wuT"], p["wz"]) for x2d in xs]
    uT3 = [uT.reshape(D_MODEL, uT.shape[1] // S5_T, S5_T) for uT, _ in proj]
    yT3 = _s5(uT3[0], uT3[1], p["lag"], p["pm"], p["qm"], p["a1"], p["a2"], p["dexp"],
              nchunk=L // S5_T)
    out = []
    for x2d, (_, zc), y3 in zip(xs, proj, yT3):
        y = jnp.transpose(y3, (1, 2, 0)).reshape(x2d.shape)
        out.append(_out_odd(y, zc, x2d, p["wg"], p["bg"], p["wo"], fw, final=final))
    return out


def kernel(x_prompt, x_sample, norm_w, final_norm_w, rel_bias, w_in_ab, conv_w, conv_b,
           ssd_dt_bias, ssd_A_log, ssd_D, ssd_norm_w, diff_lambda, diff_subln_w, w_out_ab,
           w_in_c, s5_lambda_re, s5_lambda_im, s5_log_dt, s5_B_re, s5_B_im, s5_C_re, s5_C_im,
           s5_D, w_glu, b_glu, w_out_c):
    even = [_prep_even(e, w_in_ab, conv_w, conv_b, ssd_dt_bias, ssd_A_log, ssd_D, ssd_norm_w,
                       diff_lambda, diff_subln_w, w_out_ab) for e in range((DEPTH + 1) // 2)]
    s5w = jax.vmap(_s5_weights)(s5_lambda_re, s5_lambda_im, s5_log_dt, s5_B_re, s5_B_im,
                                s5_C_re, s5_C_im)
    odd = [_prep_odd(o, s5w, w_in_c, s5_D, w_glu, b_glu, w_out_c) for o in range(DEPTH // 2)]
    fw = final_norm_w[None, :]
    L = x_prompt.shape[1]
    assert x_sample.shape[1] == L
    bt = _bias_tiles(rel_bias, L)
    shapes = [x_prompt.shape, x_sample.shape]
    xs = [x.reshape(-1, D_MODEL) for x in (x_prompt, x_sample)]
    for l in range(DEPTH):
        nw = norm_w[l][None, :]
        if l % 2 == 0:
            lambda_init = 0.8 - 0.6 * math.exp(-0.3 * l)
            xs = [_even_layer(x2d, nw, even[l // 2], bt, nb=shp[0], L=L, lambda_init=lambda_init)
                  for x2d, shp in zip(xs, shapes)]
        else:
            xs = _odd_layer(xs, nw, odd[l // 2], fw, L=L, final=(l == DEPTH - 1))
    return tuple(x2d.reshape(shp) for x2d, shp in zip(xs, shapes))
```
